```python
import math
import numpy as np
import jax
import jax.numpy as jnp
from jax import lax

D_MODEL = 1024
BATCH = 2
SEQ = 8192
DEPTH = 4

N_A_LAYERS = DEPTH // 2
N_B_LAYERS = DEPTH - N_A_LAYERS

NSA_HEADS = 16
NSA_KV_HEADS = 4
NSA_GROUP = NSA_HEADS // NSA_KV_HEADS
HEAD_DIM = 64
CMP_BLOCK = 32
CMP_STRIDE = 16
SEL_BLOCK = 64
SEL_TOP_N = 16
WINDOW = 512
PHI_HIDDEN = 256
NSA_Q_W = NSA_HEADS * HEAD_DIM
NSA_KV_W = NSA_KV_HEADS * HEAD_DIM
NSA_IN_W = NSA_Q_W + 6 * NSA_KV_W + 3 * NSA_HEADS

DIFF_HEADS = 8
DIFF_QK_DIM = 64
DIFF_V_DIM = 128
DIFF_Q_W = DIFF_HEADS * 2 * DIFF_QK_DIM
DIFF_V_W = DIFF_HEADS * DIFF_V_DIM

ROT_DIM = HEAD_DIM // 4
ROPE_THETA = 500000.0

FFN_DIM = 2752
CONV_WIDTH = 3

Q_BLOCK = 128
EPS = 1e-6
NEG = -1e30
FORCE = 1e4

kernel_name = "yoco_nsa_diffattn_convglu_trunk"


def rmsnorm(x, g):
    xf = x.astype(jnp.float32)
    y = xf * lax.rsqrt(jnp.mean(xf * xf, axis=-1, keepdims=True) + EPS)
    return (y * g.astype(jnp.float32)).astype(x.dtype)


def partial_rope(x, pos):
    half = ROT_DIM // 2
    inv = ROPE_THETA ** (-jnp.arange(half, dtype=jnp.float32) / half)
    ang = pos.astype(jnp.float32)[:, None] * inv[None, :]
    cos = jnp.cos(ang)[None, :, None, :].astype(x.dtype)
    sin = jnp.sin(ang)[None, :, None, :].astype(x.dtype)
    x1 = x[..., :half]
    x2 = x[..., half:ROT_DIM]
    return jnp.concatenate([x1 * cos - x2 * sin, x2 * cos + x1 * sin, x[..., ROT_DIM:]], axis=-1)


def masked_softmax(s, mask, axis=-1):
    s = jnp.where(mask, s.astype(jnp.float32), NEG)
    p = jax.nn.softmax(s, axis=axis)
    return jnp.where(mask, p, 0.0)


def compress(kv, pe, w1, b1, w2):
    B, S, Hk, d = kv.shape
    ch = kv.reshape(B, S // CMP_STRIDE, CMP_STRIDE, Hk, d)
    blocks = jnp.concatenate([ch[:, :-1], ch[:, 1:]], axis=2)
    blocks = blocks + pe[None, None, :, None, :]
    nc = blocks.shape[1]
    flat = blocks.transpose(0, 1, 3, 2, 4).reshape(B, nc, Hk, CMP_BLOCK * d)
    hid = jax.nn.gelu(flat @ w1 + b1)
    return hid @ w2


def nsa_mixer(h, w_in, w_out, phi_pe, phi_w1, phi_b1, phi_w2):
    B, S, _ = h.shape
    Hk, G, hd = NSA_KV_HEADS, NSA_GROUP, HEAD_DIM
    proj = h @ w_in
    q = proj[..., :NSA_Q_W].reshape(B, S, NSA_HEADS, hd)
    kv6 = proj[..., NSA_Q_W:NSA_Q_W + 6 * NSA_KV_W].reshape(B, S, 6, Hk, hd)
    kc, vc, ks, vs, kw, vw = (kv6[:, :, i] for i in range(6))
    gates = jax.nn.sigmoid(proj[..., NSA_Q_W + 6 * NSA_KV_W:].astype(jnp.float32)).astype(h.dtype)
    gates = gates.reshape(B, S, NSA_HEADS, 3)

    pos = jnp.arange(S)
    qr = partial_rope(q, pos)
    ks = partial_rope(ks, pos)
    kw = partial_rope(kw, pos)

    k_cmp = compress(kc, phi_pe[0], phi_w1[0], phi_b1[0], phi_w2[0])
    v_cmp = compress(vc, phi_pe[1], phi_w1[1], phi_b1[1], phi_w2[1])
    n_cmp = S // CMP_STRIDE - 1
    n_sel = S // SEL_BLOCK
    top_n = min(SEL_TOP_N, n_sel)
    cmp_end = jnp.arange(n_cmp) * CMP_STRIDE + CMP_BLOCK - 1
    ci = np.arange(n_cmp)[:, None]
    sj = np.arange(n_sel)[None, :]
    overlap = jnp.asarray(((ci * CMP_STRIDE < (sj + 1) * SEL_BLOCK)
                           & (ci * CMP_STRIDE + CMP_BLOCK > sj * SEL_BLOCK)).astype(np.float32))

    ks_blocks = ks.reshape(B, n_sel, SEL_BLOCK, Hk, hd).transpose(0, 3, 1, 2, 4)
    vs_blocks = vs.reshape(B, n_sel, SEL_BLOCK, Hk, hd).transpose(0, 3, 1, 2, 4)
    kw_pad = jnp.pad(kw, ((0, 0), (WINDOW, 0), (0, 0), (0, 0)))
    vw_pad = jnp.pad(vw, ((0, 0), (WINDOW, 0), (0, 0), (0, 0)))
    scale = HEAD_DIM ** -0.5
    b_ix = jnp.arange(B)[:, None, None, None]
    h_ix = jnp.arange(Hk)[None, None, :, None]
    blk = jnp.arange(n_sel)

    def block_fn(i):
        qs = i * Q_BLOCK
        t = qs + jnp.arange(Q_BLOCK)
        qb = lax.dynamic_slice_in_dim(q, qs, Q_BLOCK, axis=1).reshape(B, Q_BLOCK, Hk, G, hd)
        qrb = lax.dynamic_slice_in_dim(qr, qs, Q_BLOCK, axis=1).reshape(B, Q_BLOCK, Hk, G, hd)
        gb = lax.dynamic_slice_in_dim(gates, qs, Q_BLOCK, axis=1).reshape(B, Q_BLOCK, Hk, G, 3)

        s_c = jnp.einsum('bqkgd,bckd->bqkgc', qb, k_cmp) * scale
        m_c = cmp_end[None, :] <= t[:, None]
        p_c = masked_softmax(s_c, m_c[None, :, None, None, :])
        o_c = jnp.einsum('bqkgc,bckd->bqkgd', p_c.astype(v_cmp.dtype), v_cmp)

        imp = jnp.einsum('bqkgc,cn->bqkn', p_c, overlap)
        cur = t // SEL_BLOCK
        valid = blk[None, :] <= cur[:, None]
        forced = (blk[None, :] == 0) | (blk[None, :] == cur[:, None]) | (blk[None, :] == cur[:, None] - 1)
        imp = jnp.where(valid[None, :, None, :],
                        jnp.where(forced[None, :, None, :], FORCE, imp), NEG)
        _, idx = lax.top_k(imp, top_n)

        k_sel = ks_blocks[b_ix, h_ix, idx]
        v_sel = vs_blocks[b_ix, h_ix, idx]
        s_s = jnp.einsum('bqkgd,bqknjd->bqkgnj', qrb, k_sel) * scale
        tok = idx[..., None] * SEL_BLOCK + jnp.arange(SEL_BLOCK)
        m_s = tok <= t[None, :, None, None, None]
        p_s = masked_softmax(s_s, m_s[:, :, :, None], axis=(-2, -1))
        o_s = jnp.einsum('bqkgnj,bqknjd->bqkgd', p_s.astype(v_sel.dtype), v_sel)

        kwb = lax.dynamic_slice_in_dim(kw_pad, qs, Q_BLOCK + WINDOW, axis=1)
        vwb = lax.dynamic_slice_in_dim(vw_pad, qs, Q_BLOCK + WINDOW, axis=1)
        kpos = qs - WINDOW + jnp.arange(Q_BLOCK + WINDOW)
        m_w = (kpos[None, :] <= t[:, None]) & (kpos[None, :] > t[:, None] - WINDOW) & (kpos[None, :] >= 0)
        s_w = jnp.einsum('bqkgd,bwkd->bqkgw', qrb, kwb) * scale
        p_w = masked_softmax(s_w, m_w[None, :, None, None, :])
        o_w = jnp.einsum('bqkgw,bwkd->bqkgd', p_w.astype(vwb.dtype), vwb)

        o = gb[..., 0:1] * o_c + gb[..., 1:2] * o_s + gb[..., 2:3] * o_w
        return o.reshape(B, Q_BLOCK, NSA_Q_W)

    out = lax.map(block_fn, jnp.arange(S // Q_BLOCK))
    out = out.transpose(1, 0, 2, 3).reshape(B, S, NSA_Q_W)
    return out @ w_out


def shared_kv(x, kv_norm_g, kv_w):
    B, S, _ = x.shape
    hkv = rmsnorm(x, kv_norm_g)
    kv = hkv @ kv_w
    k = kv[..., :DIFF_Q_W].reshape(B, S, DIFF_HEADS * 2, DIFF_QK_DIM)
    k = partial_rope(k, jnp.arange(S)).reshape(B, S, DIFF_HEADS, 2, DIFF_QK_DIM)
    v = kv[..., DIFF_Q_W:].reshape(B, S, DIFF_HEADS, DIFF_V_DIM)
    return k, v


def diff_mixer(h, w_q, lam_vec, subln_g, w_out, k, v, lam_init):
    B, S, _ = h.shape
    q = (h @ w_q).reshape(B, S, DIFF_HEADS * 2, DIFF_QK_DIM)
    q = partial_rope(q, jnp.arange(S)).reshape(B, S, DIFF_HEADS, 2, DIFF_QK_DIM)
    lv = lam_vec.astype(jnp.float32)
    lam = jnp.exp(jnp.sum(lv[0] * lv[1])) - jnp.exp(jnp.sum(lv[2] * lv[3])) + lam_init
    scale = DIFF_QK_DIM ** -0.5
    kpos = jnp.arange(S)

    def block_fn(i):
        qs = i * Q_BLOCK
        t = qs + jnp.arange(Q_BLOCK)
        qb = lax.dynamic_slice_in_dim(q, qs, Q_BLOCK, axis=1)
        s = jnp.einsum('bqhcd,bkhcd->bhcqk', qb, k) * scale
        mask = kpos[None, :] <= t[:, None]
        p = masked_softmax(s, mask[None, None, None])
        a = p[:, :, 0] - lam * p[:, :, 1]
        return jnp.einsum('bhqk,bkhd->bqhd', a.astype(v.dtype), v)

    o = lax.map(block_fn, jnp.arange(S // Q_BLOCK))
    o = jnp.moveaxis(o, 0, 1).reshape(B, S, DIFF_HEADS, DIFF_V_DIM)
    o = rmsnorm(o, subln_g) * (1.0 - lam_init)
    return o.reshape(B, S, DIFF_V_W) @ w_out


def conv_glu_ffn(h, w_in, conv_w, conv_b, w_out):
    S = h.shape[1]
    u = h @ w_in
    up = jnp.pad(u, ((0, 0), (CONV_WIDTH - 1, 0), (0, 0)))
    c = conv_b
    for tap in range(CONV_WIDTH):
        c = c + up[:, tap:tap + S] * conv_w[tap]
    gate = c[..., :FFN_DIM]
    val = c[..., FFN_DIM:]
    return (jax.nn.silu(gate) * val) @ w_out


def setup_inputs(seed: int = 0) -> dict:
    key = jax.random.key(seed)
    ks = jax.random.split(key, 24)
    f32 = jnp.float32

    def nrm(k, shape, fan):
        return jax.random.normal(k, shape, f32) * (fan ** -0.5)

    def gain(k, shape):
        return 1.0 + 0.02 * jax.random.normal(k, shape, f32)

    NA, NB = N_A_LAYERS, N_B_LAYERS
    return {
        "x": jax.random.normal(ks[0], (BATCH, SEQ, D_MODEL), f32),
        "attn_norm_g": gain(ks[1], (DEPTH, D_MODEL)),
        "ffn_norm_g": gain(ks[2], (DEPTH, D_MODEL)),
        "nsa_w_in": nrm(ks[3], (NA, D_MODEL, NSA_IN_W), D_MODEL),
        "nsa_w_out": nrm(ks[4], (NA, NSA_Q_W, D_MODEL), NSA_Q_W),
        "nsa_phi_pe": 0.1 * jax.random.normal(ks[5], (NA, 2, CMP_BLOCK, HEAD_DIM), f32),
        "nsa_phi_w1": nrm(ks[6], (NA, 2, CMP_BLOCK * HEAD_DIM, PHI_HIDDEN), CMP_BLOCK * HEAD_DIM),
        "nsa_phi_b1": 0.02 * jax.random.normal(ks[7], (NA, 2, PHI_HIDDEN), f32),
        "nsa_phi_w2": nrm(ks[8], (NA, 2, PHI_HIDDEN, HEAD_DIM), PHI_HIDDEN),
        "kv_norm_g": gain(ks[9], (D_MODEL,)),
        "kv_w": nrm(ks[10], (D_MODEL, DIFF_Q_W + DIFF_V_W), D_MODEL),
        "diff_w_q": nrm(ks[11], (NB, D_MODEL, DIFF_Q_W), D_MODEL),
        "diff_lambda": 0.1 * jax.random.normal(ks[12], (NB, 4, DIFF_QK_DIM), f32),
        "diff_subln_g": gain(ks[13], (NB, DIFF_V_DIM)),
        "diff_w_out": nrm(ks[14], (NB, DIFF_V_W, D_MODEL), DIFF_V_W),
        "ffn_w_in": nrm(ks[15], (DEPTH, D_MODEL, 2 * FFN_DIM), D_MODEL),
        "ffn_conv_w": nrm(ks[16], (DEPTH, CONV_WIDTH, 2 * FFN_DIM), CONV_WIDTH),
        "ffn_conv_b": 0.02 * jax.random.normal(ks[17], (DEPTH, 2 * FFN_DIM), f32),
        "ffn_w_out": nrm(ks[18], (DEPTH, FFN_DIM, D_MODEL), FFN_DIM),
        "final_norm_g": gain(ks[19], (D_MODEL,)),
    }


def reference(x, attn_norm_g, ffn_norm_g, nsa_w_in, nsa_w_out, nsa_phi_pe, nsa_phi_w1, nsa_phi_b1,
              nsa_phi_w2, kv_norm_g, kv_w, diff_w_q, diff_lambda, diff_subln_g, diff_w_out,
              ffn_w_in, ffn_conv_w, ffn_conv_b, ffn_w_out, final_norm_g):
    k_sh = None
    v_sh = None
    for l in range(DEPTH):
        h = rmsnorm(x, attn_norm_g[l])
        if l < N_A_LAYERS:
            x = x + nsa_mixer(h, nsa_w_in[l], nsa_w_out[l], nsa_phi_pe[l], nsa_phi_w1[l],
                              nsa_phi_b1[l], nsa_phi_w2[l])
        else:
            j = l - N_A_LAYERS
            lam_init = 0.8 - 0.6 * math.exp(-0.3 * l)
            x = x + diff_mixer(h, diff_w_q[j], diff_lambda[j], diff_subln_g[j], diff_w_out[j],
                               k_sh, v_sh, lam_init)
        h = rmsnorm(x, ffn_norm_g[l])
        x = x + conv_glu_ffn(h, ffn_w_in[l], ffn_conv_w[l], ffn_conv_b[l], ffn_w_out[l])
        if l == N_A_LAYERS - 1:
            k_sh, v_sh = shared_kv(x, kv_norm_g, kv_w)
    return rmsnorm(x, final_norm_g)
```

```python
import functools
import math

import jax
import jax.numpy as jnp
from jax import lax
from jax.experimental import pallas as pl
from jax.experimental.pallas import tpu as pltpu

D_MODEL = 1024
DEPTH = 4
N_A_LAYERS = DEPTH // 2
NSA_HEADS = 16
NSA_KV_HEADS = 4
NSA_GROUP = NSA_HEADS // NSA_KV_HEADS
HEAD_DIM = 64
CMP_BLOCK = 32
CMP_STRIDE = 16
SEL_BLOCK = 64
SEL_TOP_N = 16
WINDOW = 512
PHI_HIDDEN = 256
NSA_Q_W = NSA_HEADS * HEAD_DIM
NSA_KV_W = NSA_KV_HEADS * HEAD_DIM
DIFF_HEADS = 8
DIFF_QK_DIM = 64
DIFF_V_DIM = 128
DIFF_Q_W = DIFF_HEADS * 2 * DIFF_QK_DIM
DIFF_V_W = DIFF_HEADS * DIFF_V_DIM
ROT_DIM = HEAD_DIM // 4
ROPE_THETA = 500000.0
FFN_DIM = 2752
CONV_WIDTH = 3
EPS = 1e-6
NEG = -1e30
FORCE = 1e4

LANES = 128
MXU_DIM = 256
VMEM_LIMIT_BYTES = 56 * 1024 * 1024

COL_TILE = MXU_DIM
ROW_TILE = 512
NSA_TQ = 128
NSA_TK = 256
WIN_TK = 128
DIFF_TQ = 256
DIFF_TK = 256
FFN_PAD = 2816
FFN_CHUNK = FFN_PAD // 2
MASK_BIAS = -1e9
REMOVED = -3.0e38

MODE_PLAIN, MODE_SCALE, MODE_ROPE, MODE_ROPE_SCALE, MODE_SIGMOID = range(5)
QK_SCALE = HEAD_DIM ** -0.5


def _cparams(sem):
    return pltpu.CompilerParams(dimension_semantics=sem, vmem_limit_bytes=VMEM_LIMIT_BYTES)


def _rms(x, g):
    y = x * lax.rsqrt(jnp.mean(x * x, axis=-1, keepdims=True) + EPS)
    return y * g


def _dot(a, b):
    return jnp.dot(a, b, preferred_element_type=jnp.float32)


def _norm_linear_kernel(x_ref, g_ref, w_ref, rope_ref, o_ref, *, modes):
    h = _rms(x_ref[...], g_ref[...]).astype(jnp.bfloat16)
    for j, mode in enumerate(modes):
        c0 = j * COL_TILE
        y = _dot(h, w_ref[:, c0:c0 + COL_TILE])
        if mode in (MODE_ROPE, MODE_ROPE_SCALE):
            cos, s_lo, s_hi = rope_ref[0], rope_ref[1], rope_ref[2]
            parts = []
            for c in range(COL_TILE // LANES):
                yh = y[:, c * LANES:(c + 1) * LANES]
                up = pltpu.roll(yh, LANES - ROT_DIM // 2, 1)
                dn = pltpu.roll(yh, ROT_DIM // 2, 1)
                parts.append(yh * cos + up * s_lo + dn * s_hi)
            y = jnp.concatenate(parts, axis=1)
        if mode in (MODE_SCALE, MODE_ROPE_SCALE):
            y = y * QK_SCALE
        if mode == MODE_SIGMOID:
            y = jax.nn.sigmoid(y)
        o_ref[:, c0:c0 + COL_TILE] = y.astype(o_ref.dtype)


def norm_linear(x2d, g, w, modes, rope_tab, seq):
    m, d = x2d.shape
    n = w.shape[1]
    assert n == len(modes) * COL_TILE and m % ROW_TILE == 0 and seq % ROW_TILE == 0
    tiles_per_seq = seq // ROW_TILE
    return pl.pallas_call(
        functools.partial(_norm_linear_kernel, modes=tuple(modes)),
        out_shape=jax.ShapeDtypeStruct((m, n), jnp.bfloat16),
        grid=(m // ROW_TILE,),
        in_specs=[
            pl.BlockSpec((ROW_TILE, d), lambda i: (i, 0)),
            pl.BlockSpec((1, d), lambda i: (0, 0)),
            pl.BlockSpec((d, n), lambda i: (0, 0)),
            pl.BlockSpec((3, ROW_TILE, LANES), lambda i: (0, i % tiles_per_seq, 0)),
        ],
        out_specs=pl.BlockSpec((ROW_TILE, n), lambda i: (i, 0)),
        compiler_params=_cparams(("parallel",)),
        name="norm_linear",
    )(x2d, g.reshape(1, d), w, rope_tab)


def _res_linear_kernel(x_ref, a_ref, w_ref, o_ref):
    o_ref[...] = x_ref[...] + _dot(a_ref[...], w_ref[...])


def res_linear(x2d, a2d, w):
    m, d = x2d.shape
    k = a2d.shape[1]
    return pl.pallas_call(
        _res_linear_kernel,
        out_shape=jax.ShapeDtypeStruct((m, d), jnp.float32),
        grid=(m // ROW_TILE,),
        in_specs=[
            pl.BlockSpec((ROW_TILE, d), lambda i: (i, 0)),
            pl.BlockSpec((ROW_TILE, k), lambda i: (i, 0)),
            pl.BlockSpec((k, d), lambda i: (0, 0)),
        ],
        out_specs=pl.BlockSpec((ROW_TILE, d), lambda i: (i, 0)),
        compiler_params=_cparams(("parallel",)),
        name="res_linear",
    )(x2d, a2d, w)


def _compress_kernel(ch_ref, pe_ref, w1_ref, b1_ref, w2_ref, o_ref):
    ch = ch_ref[0, 0, 0]
    w1 = w1_ref[0]
    half = CMP_STRIDE * HEAD_DIM
    nc = ch.shape[0]
    first = _dot(ch, w1[:half])
    second = _dot(ch, w1[half:])
    second = pltpu.roll(second, nc - 1, 0)
    bias = _dot(pe_ref[0], w1)[0:1] + b1_ref[0]
    pre = first + second + bias
    hid = jax.nn.gelu(pre, approximate=True)
    o_ref[0, 0, 0] = _dot(hid.astype(jnp.bfloat16), w2_ref[0]).astype(o_ref.dtype)


def compress(chunks, pe_flat8, w1, b1, w2):
    _, b, hk, nc, cw = chunks.shape
    return pl.pallas_call(
        _compress_kernel,
        out_shape=jax.ShapeDtypeStruct((2, b, hk, nc, HEAD_DIM), jnp.bfloat16),
        grid=(2, b, hk),
        in_specs=[
            pl.BlockSpec((1, 1, 1, nc, cw), lambda s, i, j: (s, i, j, 0, 0)),
            pl.BlockSpec((1, 8, 2 * cw), lambda s, i, j: (s, 0, 0)),
            pl.BlockSpec((1, 2 * cw, PHI_HIDDEN), lambda s, i, j: (s, 0, 0)),
            pl.BlockSpec((1, 1, PHI_HIDDEN), lambda s, i, j: (s, 0, 0)),
            pl.BlockSpec((1, PHI_HIDDEN, HEAD_DIM), lambda s, i, j: (s, 0, 0)),
        ],
        out_specs=pl.BlockSpec((1, 1, 1, nc, HEAD_DIM), lambda s, i, j: (s, i, j, 0, 0)),
        compiler_params=_cparams(("parallel", "parallel", "parallel")),
        name="nsa_compress",
    )(chunks, pe_flat8, w1, b1, w2)


def _nsa_attn_kernel(qT_ref, qrT_ref, kc_ref, vcT_ref, ovT_ref, kaug_ref, vsT_ref, kw_ref, vwT_ref,
                     gate_ref, o_ref, m_s, l_s, acc_s, *, n_sel, top_n):
    g_heads = NSA_GROUP
    tq = NSA_TQ
    cols = g_heads * tq
    i = pl.program_id(2)
    qs = i * tq
    qT = jnp.concatenate([qT_ref[0, g] for g in range(g_heads)], axis=1)
    qrT = jnp.concatenate([qrT_ref[0, g] for g in range(g_heads)], axis=1)
    t_col = qs + (lax.broadcasted_iota(jnp.int32, (1, cols), 1) & (tq - 1))

    kc = kc_ref[0, 0]
    nc = kc.shape[0]
    sc = _dot(kc, qT)
    c_end = lax.broadcasted_iota(jnp.int32, (nc, 1), 0) * CMP_STRIDE + (CMP_BLOCK - 1)
    cmask = c_end <= t_col
    sc = jnp.where(cmask, sc, NEG)
    mc = jnp.max(sc, axis=0, keepdims=True)
    ec = jnp.where(cmask, jnp.exp(sc - mc), 0.0)
    lc = jnp.sum(ec, axis=0, keepdims=True)
    pc = ec * jnp.where(lc > 0.0, 1.0 / lc, 0.0)
    o_c = _dot(vcT_ref[0, 0], pc.astype(jnp.bfloat16))

    psum = pc[:, 0:tq]
    for g in range(1, g_heads):
        psum = psum + pc[:, g * tq:(g + 1) * tq]
    p_hi = psum.astype(jnp.bfloat16)
    p_lo = (psum - p_hi.astype(jnp.float32)).astype(jnp.bfloat16)
    ovT = ovT_ref[...]
    imp = _dot(ovT, p_hi) + _dot(ovT, p_lo)

    blk = lax.broadcasted_iota(jnp.int32, (n_sel, tq), 0)
    blk_f = blk.astype(jnp.float32)
    cur = (qs + lax.broadcasted_iota(jnp.int32, (1, tq), 1)) // SEL_BLOCK
    valid = blk <= cur
    work = jnp.where(blk == 0, FORCE, jnp.where(blk == cur, FORCE, jnp.where(blk == cur - 1, FORCE, imp)))
    work = jnp.where(valid, work, NEG)
    for _ in range(top_n):
        top = jnp.max(work, axis=0, keepdims=True)
        first = jnp.min(jnp.where(work == top, blk_f, float(n_sel)), axis=0, keepdims=True)
        work = jnp.where(blk_f == first, REMOVED, work)
    selbias = jnp.where(valid, jnp.where(work == REMOVED, 0.0, MASK_BIAS), MASK_BIAS)
    selbias = selbias.astype(jnp.bfloat16)
    q_aug = jnp.concatenate(
        [qrT, jnp.zeros((LANES - HEAD_DIM, cols), jnp.bfloat16),
         jnp.concatenate([selbias] * g_heads, axis=1)], axis=0)

    m_s[...] = jnp.full(m_s.shape, NEG, jnp.float32)
    l_s[...] = jnp.zeros(l_s.shape, jnp.float32)
    acc_s[...] = jnp.zeros(acc_s.shape, jnp.float32)

    def sel_step(kt, causal):
        s = _dot(kaug_ref[0, 0, kt], q_aug)
        if causal:
            kpos = kt * NSA_TK + lax.broadcasted_iota(jnp.int32, (NSA_TK, 1), 0)
            s = jnp.where(kpos <= t_col, s, MASK_BIAS)
        m_old = m_s[...]
        m_new = jnp.maximum(m_old, jnp.max(s, axis=0, keepdims=True))
        alpha = jnp.exp(m_old - m_new)
        p = jnp.exp(s - m_new)
        l_s[...] = alpha * l_s[...] + jnp.sum(p, axis=0, keepdims=True)
        acc_s[...] = alpha * acc_s[...] + _dot(vsT_ref[0, 0, kt], p.astype(jnp.bfloat16))
        m_s[...] = m_new

    n_full = (qs + tq - 1) // NSA_TK

    def body(kt, carry):
        sel_step(kt, False)
        return carry

    lax.fori_loop(0, n_full, body, 0)
    sel_step(n_full, True)
    o_s = acc_s[...] * (1.0 / l_s[...])

    n_win = WINDOW // WIN_TK + 1
    w0 = jnp.maximum(i - WINDOW // WIN_TK, 0)
    s_tiles = []
    for w in range(n_win):
        s = _dot(kw_ref[0, 0, w0 + w], qrT)
        kpos = (w0 + w) * WIN_TK + lax.broadcasted_iota(jnp.int32, (WIN_TK, 1), 0)
        s = jnp.where(kpos > t_col - WINDOW, s, MASK_BIAS)
        s_tiles.append(jnp.where(kpos <= t_col, s, MASK_BIAS))
    mw = s_tiles[0].max(axis=0, keepdims=True)
    for s in s_tiles[1:]:
        mw = jnp.maximum(mw, s.max(axis=0, keepdims=True))
    lw = jnp.zeros((1, cols), jnp.float32)
    o_w = jnp.zeros((HEAD_DIM, cols), jnp.float32)
    for w in range(n_win):
        p = jnp.exp(s_tiles[w] - mw)
        lw = lw + jnp.sum(p, axis=0, keepdims=True)
        o_w = o_w + _dot(vwT_ref[0, 0, w0 + w], p.astype(jnp.bfloat16))
    o_w = o_w * (1.0 / lw)

    gate = gate_ref[0, 0, 0]
    out = gate[0:1] * o_c + gate[1:2] * o_s + gate[2:3] * o_w
    o_ref[0, 0, 0] = out.astype(o_ref.dtype)


def nsa_attention(qT, qrT, kcmp, vcmpT, ovT, kaug, vsT, kw, vwT, gates, seq):
    b = qT.shape[0]
    hk = NSA_KV_HEADS
    nq = seq // NSA_TQ
    n_sel = seq // SEL_BLOCK
    nc = kcmp.shape[2]
    cols = NSA_GROUP * NSA_TQ
    kd = kaug.shape[-1]
    nkt = seq // NSA_TK
    nwt = seq // WIN_TK
    kern = functools.partial(_nsa_attn_kernel, n_sel=n_sel, top_n=min(SEL_TOP_N, n_sel))
    return pl.pallas_call(
        kern,
        out_shape=jax.ShapeDtypeStruct((b, hk, nq, HEAD_DIM, cols), jnp.bfloat16),
        grid=(b, hk, nq),
        in_specs=[
            pl.BlockSpec((1, NSA_GROUP, HEAD_DIM, NSA_TQ), lambda bi, h, i: (bi, h, 0, i)),
            pl.BlockSpec((1, NSA_GROUP, HEAD_DIM, NSA_TQ), lambda bi, h, i: (bi, h, 0, i)),
            pl.BlockSpec((1, 1, nc, HEAD_DIM), lambda bi, h, i: (bi, h, 0, 0)),
            pl.BlockSpec((1, 1, HEAD_DIM, nc), lambda bi, h, i: (bi, h, 0, 0)),
            pl.BlockSpec((n_sel, nc), lambda bi, h, i: (0, 0)),
            pl.BlockSpec((1, 1, nkt, NSA_TK, kd), lambda bi, h, i: (bi, h, 0, 0, 0)),
            pl.BlockSpec((1, 1, nkt, HEAD_DIM, NSA_TK), lambda bi, h, i: (bi, h, 0, 0, 0)),
            pl.BlockSpec((1, 1, nwt, WIN_TK, HEAD_DIM), lambda bi, h, i: (bi, h, 0, 0, 0)),
            pl.BlockSpec((1, 1, nwt, HEAD_DIM, WIN_TK), lambda bi, h, i: (bi, h, 0, 0, 0)),
            pl.BlockSpec((1, 1, 1, 3, cols), lambda bi, h, i: (bi, h, i, 0, 0)),
        ],
        out_specs=pl.BlockSpec((1, 1, 1, HEAD_DIM, cols), lambda bi, h, i: (bi, h, i, 0, 0)),
        scratch_shapes=[
            pltpu.VMEM((1, cols), jnp.float32),
            pltpu.VMEM((1, cols), jnp.float32),
            pltpu.VMEM((HEAD_DIM, cols), jnp.float32),
        ],
        compiler_params=_cparams(("parallel", "parallel", "arbitrary")),
        name="nsa_attention",
    )(qT, qrT, kcmp, vcmpT, ovT, kaug, vsT, kw, vwT, gates)


def _diff_attn_kernel(qT_ref, k_ref, vT_ref, lam_ref, g_ref, o_ref, m_s, l_s, acc_s, *, lam_init):
    tq = DIFF_TQ
    cols = 2 * tq
    i = pl.program_id(2)
    qs = i * tq
    q2 = qT_ref[0, 0]
    zero = jnp.zeros((DIFF_QK_DIM, tq), jnp.bfloat16)
    q_bd = jnp.concatenate(
        [jnp.concatenate([q2[:DIFF_QK_DIM], zero], axis=1),
         jnp.concatenate([zero, q2[DIFF_QK_DIM:]], axis=1)], axis=0)
    t_col = qs + (lax.broadcasted_iota(jnp.int32, (1, cols), 1) & (tq - 1))

    m_s[...] = jnp.full(m_s.shape, NEG, jnp.float32)
    l_s[...] = jnp.zeros(l_s.shape, jnp.float32)
    acc_s[...] = jnp.zeros(acc_s.shape, jnp.float32)

    def step(kt, causal):
        s = _dot(k_ref[0, 0, kt], q_bd)
        if causal:
            kpos = kt * DIFF_TK + lax.broadcasted_iota(jnp.int32, (DIFF_TK, 1), 0)
            s = jnp.where(kpos <= t_col, s, MASK_BIAS)
        m_old = m_s[...]
        m_new = jnp.maximum(m_old, jnp.max(s, axis=0, keepdims=True))
        alpha = jnp.exp(m_old - m_new)
        p = jnp.exp(s - m_new)
        l_s[...] = alpha * l_s[...] + jnp.sum(p, axis=0, keepdims=True)
        acc_s[...] = alpha * acc_s[...] + _dot(vT_ref[0, 0, kt], p.astype(jnp.bfloat16))
        m_s[...] = m_new

    def body(kt, carry):
        step(kt, False)
        return carry

    lax.fori_loop(0, i, body, 0)
    step(i, True)

    lv = lam_ref[...]
    lam = (jnp.exp(jnp.sum(lv[0:1] * lv[1:2], axis=1, keepdims=True))
           - jnp.exp(jnp.sum(lv[2:3] * lv[3:4], axis=1, keepdims=True)) + lam_init)
    o = acc_s[...] * (1.0 / l_s[...])
    a = o[:, :tq] - lam * o[:, tq:]
    y = a * lax.rsqrt(jnp.mean(a * a, axis=0, keepdims=True) + EPS)
    g = jnp.concatenate([g_ref[...]] * (tq // LANES), axis=1)
    y = (y * g) * (1.0 - lam_init)
    o_ref[0, 0, 0] = y.astype(o_ref.dtype)


def diff_attention(qT, kcat, vT, lam_vec, g_rows, lam_init, seq):
    b = qT.shape[0]
    nq = seq // DIFF_TQ
    nkt = seq // DIFF_TK
    cols = 2 * DIFF_TQ
    return pl.pallas_call(
        functools.partial(_diff_attn_kernel, lam_init=lam_init),
        out_shape=jax.ShapeDtypeStruct((b, DIFF_HEADS, nq, DIFF_V_DIM, DIFF_TQ), jnp.bfloat16),
        grid=(b, DIFF_HEADS, nq),
        in_specs=[
            pl.BlockSpec((1, 1, 2 * DIFF_QK_DIM, DIFF_TQ), lambda bi, h, i: (bi, h, 0, i)),
            pl.BlockSpec((1, 1, nkt, DIFF_TK, 2 * DIFF_QK_DIM), lambda bi, h, i: (bi, h, 0, 0, 0)),
            pl.BlockSpec((1, 1, nkt, DIFF_V_DIM, DIFF_TK), lambda bi, h, i: (bi, h, 0, 0, 0)),
            pl.BlockSpec((4, DIFF_QK_DIM), lambda bi, h, i: (0, 0)),
            pl.BlockSpec((DIFF_V_DIM, LANES), lambda bi, h, i: (0, 0)),
        ],
        out_specs=pl.BlockSpec((1, 1, 1, DIFF_V_DIM, DIFF_TQ), lambda bi, h, i: (bi, h, i, 0, 0)),
        scratch_shapes=[
            pltpu.VMEM((1, cols), jnp.float32),
            pltpu.VMEM((1, cols), jnp.float32),
            pltpu.VMEM((DIFF_V_DIM, cols), jnp.float32),
        ],
        compiler_params=_cparams(("parallel", "parallel", "arbitrary")),
        name="diff_attention",
    )(qT, kcat, vT, lam_vec, g_rows)


HALO = 8


def _ffn_kernel(x_ref, xh_ref, g_ref, wg_ref, wv_ref, cwg_ref, cwv_ref, cbg_ref, cbv_ref, wo_ref, fg_ref,
                o_ref, h_s, acc_s, *, tiles_per_seq, final_norm):
    i = pl.program_id(0)
    f = pl.program_id(1)
    tm = x_ref.shape[0]

    @pl.when(f == 0)
    def _():
        g = g_ref[...]
        h_s[HALO:, :] = _rms(x_ref[...], g).astype(jnp.bfloat16)
        hh = _rms(xh_ref[...], g)
        hh = jnp.where(i % tiles_per_seq == 0, 0.0, hh)
        h_s[:HALO, :] = hh.astype(jnp.bfloat16)
        acc_s[...] = jnp.zeros(acc_s.shape, jnp.float32)

    h = h_s[...]

    def conv(u, cw_ref, cb_ref):
        cw = cw_ref[...]
        c = cb_ref[...] + u[HALO:] * cw[2:3]
        c = c + pltpu.roll(u, 1, 0)[HALO:] * cw[1:2]
        c = c + pltpu.roll(u, 2, 0)[HALO:] * cw[0:1]
        return c

    gate = conv(_dot(h, wg_ref[...]), cwg_ref, cbg_ref)
    val = conv(_dot(h, wv_ref[...]), cwv_ref, cbv_ref)
    act = (gate * jax.nn.sigmoid(gate)) * val
    acc_s[...] += _dot(act.astype(jnp.bfloat16), wo_ref[...])

    @pl.when(f == pl.num_programs(1) - 1)
    def _():
        y = x_ref[...] + acc_s[...]
        if final_norm:
            y = _rms(y, fg_ref[...])
        o_ref[...] = y


def ffn(x2d, g, wg, wv, cwg, cwv, cbg, cbv, wo, final_g, seq):
    m, d = x2d.shape
    tm = ROW_TILE
    nf = FFN_PAD // FFN_CHUNK
    tiles_per_seq = seq // tm
    final_norm = final_g is not None
    fg = (final_g if final_norm else g).reshape(1, d)
    kern = functools.partial(_ffn_kernel, tiles_per_seq=tiles_per_seq, final_norm=final_norm)
    return pl.pallas_call(
        kern,
        out_shape=jax.ShapeDtypeStruct((m, d), jnp.float32),
        grid=(m // tm, nf),
        in_specs=[
            pl.BlockSpec((tm, d), lambda i, f: (i, 0)),
            pl.BlockSpec((HALO, d), lambda i, f: (jnp.maximum(i * (tm // HALO) - 1, 0), 0)),
            pl.BlockSpec((1, d), lambda i, f: (0, 0)),
            pl.BlockSpec((d, FFN_CHUNK), lambda i, f: (0, f)),
            pl.BlockSpec((d, FFN_CHUNK), lambda i, f: (0, f)),
            pl.BlockSpec((CONV_WIDTH, FFN_CHUNK), lambda i, f: (0, f)),
            pl.BlockSpec((CONV_WIDTH, FFN_CHUNK), lambda i, f: (0, f)),
            pl.BlockSpec((1, FFN_CHUNK), lambda i, f: (0, f)),
            pl.BlockSpec((1, FFN_CHUNK), lambda i, f: (0, f)),
            pl.BlockSpec((FFN_CHUNK, d), lambda i, f: (f, 0)),
            pl.BlockSpec((1, d), lambda i, f: (0, 0)),
        ],
        out_specs=pl.BlockSpec((tm, d), lambda i, f: (i, 0)),
        scratch_shapes=[
            pltpu.VMEM((HALO + tm, d), jnp.bfloat16),
            pltpu.VMEM((tm, d), jnp.float32),
        ],
        compiler_params=_cparams(("parallel", "arbitrary")),
        name="conv_glu_ffn",
    )(x2d, x2d, g.reshape(1, d), wg, wv, cwg, cwv, cbg, cbv, wo, fg)


def _rope_tables(seq):
    half = ROT_DIM // 2
    inv = ROPE_THETA ** (-jnp.arange(half, dtype=jnp.float32) / half)
    ang = jnp.arange(seq).astype(jnp.float32)[:, None] * inv[None, :]
    cos, sin = jnp.cos(ang), jnp.sin(ang)
    ones = jnp.ones((seq, HEAD_DIM - ROT_DIM), jnp.float32)
    zeros = jnp.zeros((seq, HEAD_DIM - ROT_DIM), jnp.float32)
    z8 = jnp.zeros((seq, half), jnp.float32)
    c = jnp.concatenate([cos, cos, ones], axis=1)
    s_lo = jnp.concatenate([-sin, z8, zeros], axis=1)
    s_hi = jnp.concatenate([z8, sin, zeros], axis=1)
    tab = jnp.stack([c, s_lo, s_hi])
    return jnp.concatenate([tab] * (LANES // HEAD_DIM), axis=2)


def _pad_cols(w, n):
    return jnp.pad(w, ((0, 0), (0, n - w.shape[1])))


def _nsa_layer(x2d, bsz, seq, g, w_in, w_out, pe, w1, b1, w2, rope_tab):
    bf = jnp.bfloat16
    hk, hd = NSA_KV_HEADS, HEAD_DIM
    wq = w_in[:, :NSA_Q_W]
    wkv = [w_in[:, NSA_Q_W + k * NSA_KV_W:NSA_Q_W + (k + 1) * NSA_KV_W] for k in range(6)]
    wg = _pad_cols(w_in[:, NSA_Q_W + 6 * NSA_KV_W:], COL_TILE)
    w_all = jnp.concatenate([wq, wq] + wkv + [wg], axis=1).astype(bf)
    nq_t = NSA_Q_W // COL_TILE
    modes = ([MODE_SCALE] * nq_t + [MODE_ROPE_SCALE] * nq_t
             + [MODE_PLAIN, MODE_PLAIN, MODE_ROPE, MODE_PLAIN, MODE_ROPE, MODE_PLAIN, MODE_SIGMOID])
    proj = norm_linear(x2d, g, w_all, modes, rope_tab, seq).reshape(bsz, seq, -1)

    off = 0
    q = proj[..., off:off + NSA_Q_W]; off += NSA_Q_W
    qr = proj[..., off:off + NSA_Q_W]; off += NSA_Q_W
    kc, vc, ks, vs, kw, vw = (proj[..., off + k * NSA_KV_W:off + (k + 1) * NSA_KV_W] for k in range(6))
    off += 6 * NSA_KV_W
    gates = proj[..., off:off + 3 * NSA_HEADS]

    qT = q.reshape(bsz, seq, NSA_HEADS, hd).transpose(0, 2, 3, 1)
    qrT = qr.reshape(bsz, seq, NSA_HEADS, hd).transpose(0, 2, 3, 1)

    nch = seq // CMP_STRIDE
    def chunked(a):
        return a.reshape(bsz, nch, CMP_STRIDE, hk, hd).transpose(0, 3, 1, 2, 4).reshape(bsz, hk, nch, CMP_STRIDE * hd)
    chunks = jnp.stack([chunked(kc), chunked(vc)])
    pe_flat8 = jnp.broadcast_to(pe.reshape(2, 1, CMP_BLOCK * hd), (2, 8, CMP_BLOCK * hd)).astype(bf)
    cmp = compress(chunks, pe_flat8, w1.astype(bf), b1.reshape(2, 1, PHI_HIDDEN), w2.astype(bf))
    kcmp = cmp[0]
    vcmpT = cmp[1].transpose(0, 1, 3, 2)

    n_sel = seq // SEL_BLOCK
    ci = jnp.arange(nch)[None, :]
    sj = jnp.arange(n_sel)[:, None]
    ovT = ((ci * CMP_STRIDE < (sj + 1) * SEL_BLOCK) & (ci * CMP_STRIDE + CMP_BLOCK > sj * SEL_BLOCK)
           & (ci < nch - 1)).astype(bf)

    ks_h = ks.reshape(bsz, seq, hk, hd).transpose(0, 2, 1, 3)
    onehot = (jnp.arange(seq)[:, None] // SEL_BLOCK == jnp.arange(n_sel)[None, :]).astype(bf)
    kaug = jnp.concatenate(
        [ks_h, jnp.zeros((bsz, hk, seq, LANES - hd), bf),
         jnp.broadcast_to(onehot, (bsz, hk, seq, n_sel))], axis=-1)
    kaug = kaug.reshape(bsz, hk, seq // NSA_TK, NSA_TK, LANES + n_sel)
    vsT = vs.reshape(bsz, seq // NSA_TK, NSA_TK, hk, hd).transpose(0, 3, 1, 4, 2)
    kw_t = kw.reshape(bsz, seq // WIN_TK, WIN_TK, hk, hd).transpose(0, 3, 1, 2, 4)
    vwT = vw.reshape(bsz, seq // WIN_TK, WIN_TK, hk, hd).transpose(0, 3, 1, 4, 2)
    nq = seq // NSA_TQ
    gates_t = gates.reshape(bsz, nq, NSA_TQ, hk, NSA_GROUP, 3).transpose(0, 3, 1, 5, 4, 2)
    gates_t = gates_t.reshape(bsz, hk, nq, 3, NSA_GROUP * NSA_TQ).astype(jnp.float32)

    o = nsa_attention(qT, qrT, kcmp, vcmpT, ovT, kaug, vsT, kw_t, vwT, gates_t, seq)
    o = o.reshape(bsz, hk, nq, hd, NSA_GROUP, NSA_TQ).transpose(0, 2, 5, 1, 4, 3)
    o = o.reshape(bsz * seq, NSA_Q_W)
    return res_linear(x2d, o, w_out.astype(bf))


def _shared_kv(x2d, bsz, seq, g, kv_w, rope_tab):
    nk = DIFF_Q_W // COL_TILE
    nv = DIFF_V_W // COL_TILE
    kv = norm_linear(x2d, g, kv_w.astype(jnp.bfloat16), [MODE_ROPE] * nk + [MODE_PLAIN] * nv, rope_tab, seq)
    kv = kv.reshape(bsz, seq, -1)
    k = kv[..., :DIFF_Q_W].reshape(bsz, seq // DIFF_TK, DIFF_TK, DIFF_HEADS, 2 * DIFF_QK_DIM)
    kcat = k.transpose(0, 3, 1, 2, 4)
    v = kv[..., DIFF_Q_W:].reshape(bsz, seq // DIFF_TK, DIFF_TK, DIFF_HEADS, DIFF_V_DIM)
    vT = v.transpose(0, 3, 1, 4, 2)
    return kcat, vT


def _diff_layer(x2d, bsz, seq, g, w_q, lam_vec, subln_g, w_out, kcat, vT, lam_init, rope_tab):
    bf = jnp.bfloat16
    q = norm_linear(x2d, g, w_q.astype(bf), [MODE_ROPE_SCALE] * (DIFF_Q_W // COL_TILE), rope_tab, seq)
    qT = q.reshape(bsz, seq, DIFF_HEADS, 2 * DIFF_QK_DIM).transpose(0, 2, 3, 1)
    g_rows = jnp.broadcast_to(subln_g.astype(jnp.float32)[:, None], (DIFF_V_DIM, LANES))
    o = diff_attention(qT, kcat, vT, lam_vec.astype(jnp.float32), g_rows, lam_init, seq)
    o = o.transpose(0, 2, 4, 1, 3).reshape(bsz * seq, DIFF_V_W)
    return res_linear(x2d, o, w_out.astype(bf))


def _ffn_layer(x2d, seq, g, w_in, conv_w, conv_b, w_out, final_g):
    bf = jnp.bfloat16
    f = FFN_DIM
    wg = _pad_cols(w_in[:, :f], FFN_PAD).astype(bf)
    wv = _pad_cols(w_in[:, f:], FFN_PAD).astype(bf)
    cwg = _pad_cols(conv_w[:, :f], FFN_PAD)
    cwv = _pad_cols(conv_w[:, f:], FFN_PAD)
    cbg = _pad_cols(conv_b[None, :f], FFN_PAD)
    cbv = _pad_cols(conv_b[None, f:], FFN_PAD)
    wo = jnp.pad(w_out, ((0, FFN_PAD - f), (0, 0))).astype(bf)
    return ffn(x2d, g, wg, wv, cwg, cwv, cbg, cbv, wo, final_g, seq)


def kernel(x, attn_norm_g, ffn_norm_g, nsa_w_in, nsa_w_out, nsa_phi_pe, nsa_phi_w1, nsa_phi_b1, nsa_phi_w2,
           kv_norm_g, kv_w, diff_w_q, diff_lambda, diff_subln_g, diff_w_out, ffn_w_in, ffn_conv_w, ffn_conv_b,
           ffn_w_out, final_norm_g):
    bsz, seq, d = x.shape
    x2d = x.reshape(bsz * seq, d)
    rope_tab = _rope_tables(seq)
    kcat = vT = None
    for l in range(DEPTH):
        if l < N_A_LAYERS:
            x2d = _nsa_layer(x2d, bsz, seq, attn_norm_g[l], nsa_w_in[l], nsa_w_out[l], nsa_phi_pe[l],
                             nsa_phi_w1[l], nsa_phi_b1[l], nsa_phi_w2[l], rope_tab)
        else:
            j = l - N_A_LAYERS
            lam_init = 0.8 - 0.6 * math.exp(-0.3 * l)
            x2d = _diff_layer(x2d, bsz, seq, attn_norm_g[l], diff_w_q[j], diff_lambda[j], diff_subln_g[j],
                              diff_w_out[j], kcat, vT, lam_init, rope_tab)
        final_g = final_norm_g if l == DEPTH - 1 else None
        x2d = _ffn_layer(x2d, seq, ffn_norm_g[l], ffn_w_in[l], ffn_conv_w[l], ffn_conv_b[l], ffn_w_out[l], final_g)
        if l == N_A_LAYERS - 1:
            kcat, vT = _shared_kv(x2d, bsz, seq, kv_norm_g, kv_w, rope_tab)
    return x2d.reshape(bsz, seq, d)
```

```python
import functools
import math

import jax
import jax.numpy as jnp
from jax import lax
from jax.experimental import pallas as pl
from jax.experimental.pallas import tpu as pltpu

D_MODEL = 1024
DEPTH = 4
N_A_LAYERS = DEPTH // 2
NSA_HEADS = 16
NSA_KV_HEADS = 4
NSA_GROUP = NSA_HEADS // NSA_KV_HEADS
HEAD_DIM = 64
CMP_BLOCK = 32
CMP_STRIDE = 16
SEL_BLOCK = 64
SEL_TOP_N = 16
N_FORCED = 3
WINDOW = 512
PHI_HIDDEN = 256
NSA_Q_W = NSA_HEADS * HEAD_DIM
NSA_KV_W = NSA_KV_HEADS * HEAD_DIM
DIFF_HEADS = 8
DIFF_QK_DIM = 64
DIFF_V_DIM = 128
DIFF_Q_W = DIFF_HEADS * 2 * DIFF_QK_DIM
DIFF_V_W = DIFF_HEADS * DIFF_V_DIM
ROT_DIM = HEAD_DIM // 4
ROPE_THETA = 500000.0
FFN_DIM = 2752
CONV_WIDTH = 3
EPS = 1e-6
NEG = -1e30
FORCE = 1e4

LANES = 128
SUBLANES = 8
MXU_DIM = 256
VMEM_LIMIT_BYTES = 56 * 1024 * 1024

COL_TILE = MXU_DIM
ROW_TILE = 512
ATT_T = 256
DIFF_HPS = 2
V_AUG = SUBLANES
FFN_PAD = 2816
FFN_CHUNK = FFN_PAD // 2
MASK_BIAS = -1e9
NEG_FLOOR = -1e20
REMOVED = -3.0e38

MODE_PLAIN, MODE_SCALE, MODE_ROPE, MODE_ROPE_SCALE, MODE_SIGMOID = range(5)
LOG2E = math.log2(math.e)
QK_SCALE = HEAD_DIM ** -0.5 * LOG2E


def _cparams(sem):
    return pltpu.CompilerParams(dimension_semantics=sem, vmem_limit_bytes=VMEM_LIMIT_BYTES)


def _rms(x, g):
    y = x * lax.rsqrt(jnp.mean(x * x, axis=-1, keepdims=True) + EPS)
    return y * g


def _dot(a, b):
    return jnp.dot(a, b, preferred_element_type=jnp.float32)


def _norm_linear_kernel(x_ref, g_ref, w_ref, rope_ref, o_ref, *, modes):
    h = _rms(x_ref[...], g_ref[...]).astype(jnp.bfloat16)
    for j, mode in enumerate(modes):
        c0 = j * COL_TILE
        y = _dot(h, w_ref[:, c0:c0 + COL_TILE])
        if mode in (MODE_ROPE, MODE_ROPE_SCALE):
            cos, s_lo, s_hi = rope_ref[0], rope_ref[1], rope_ref[2]
            parts = []
            for c in range(COL_TILE // LANES):
                yh = y[:, c * LANES:(c + 1) * LANES]
                up = pltpu.roll(yh, LANES - ROT_DIM // 2, 1)
                dn = pltpu.roll(yh, ROT_DIM // 2, 1)
                parts.append(yh * cos + up * s_lo + dn * s_hi)
            y = jnp.concatenate(parts, axis=1)
        if mode in (MODE_SCALE, MODE_ROPE_SCALE):
            y = y * QK_SCALE
        if mode == MODE_SIGMOID:
            y = jax.nn.sigmoid(y)
        o_ref[:, c0:c0 + COL_TILE] = y.astype(o_ref.dtype)


def norm_linear(x2d, g, w, modes, rope_tab, seq):
    m, d = x2d.shape
    n = w.shape[1]
    assert n == len(modes) * COL_TILE and m % ROW_TILE == 0 and seq % ROW_TILE == 0
    tiles_per_seq = seq // ROW_TILE
    return pl.pallas_call(
        functools.partial(_norm_linear_kernel, modes=tuple(modes)),
        out_shape=jax.ShapeDtypeStruct((m, n), jnp.bfloat16),
        grid=(m // ROW_TILE,),
        in_specs=[
            pl.BlockSpec((ROW_TILE, d), lambda i: (i, 0)),
            pl.BlockSpec((1, d), lambda i: (0, 0)),
            pl.BlockSpec((d, n), lambda i: (0, 0)),
            pl.BlockSpec((3, ROW_TILE, LANES), lambda i: (0, i % tiles_per_seq, 0)),
        ],
        out_specs=pl.BlockSpec((ROW_TILE, n), lambda i: (i, 0)),
        compiler_params=_cparams(("parallel",)),
        name="norm_linear",
    )(x2d, g.reshape(1, d), w, rope_tab)


def _res_linear_kernel(x_ref, a_ref, w_ref, o_ref):
    o_ref[...] = x_ref[...] + _dot(a_ref[...], w_ref[...])


def res_linear(x2d, a2d, w):
    m, d = x2d.shape
    k = a2d.shape[1]
    return pl.pallas_call(
        _res_linear_kernel,
        out_shape=jax.ShapeDtypeStruct((m, d), jnp.float32),
        grid=(m // ROW_TILE,),
        in_specs=[
            pl.BlockSpec((ROW_TILE, d), lambda i: (i, 0)),
            pl.BlockSpec((ROW_TILE, k), lambda i: (i, 0)),
            pl.BlockSpec((k, d), lambda i: (0, 0)),
        ],
        out_specs=pl.BlockSpec((ROW_TILE, d), lambda i: (i, 0)),
        compiler_params=_cparams(("parallel",)),
        name="res_linear",
    )(x2d, a2d, w)


def _compress_kernel(ch_ref, pe_ref, w1_ref, b1_ref, w2_ref, o_ref):
    ch = ch_ref[0, 0, 0]
    w1 = w1_ref[0]
    half = CMP_STRIDE * HEAD_DIM
    nc = ch.shape[0]
    first = _dot(ch, w1[:half])
    second = _dot(ch, w1[half:])
    second = pltpu.roll(second, nc - 1, 0)
    bias = _dot(pe_ref[0], w1)[0:1] + b1_ref[0]
    pre = first + second + bias
    hid = jax.nn.gelu(pre, approximate=True)
    o_ref[0, 0, 0] = _dot(hid.astype(jnp.bfloat16), w2_ref[0]).astype(o_ref.dtype)


def compress(chunks, pe_flat8, w1, b1, w2):
    _, b, hk, nc, cw = chunks.shape
    return pl.pallas_call(
        _compress_kernel,
        out_shape=jax.ShapeDtypeStruct((2, b, hk, nc, HEAD_DIM), jnp.bfloat16),
        grid=(2, b, hk),
        in_specs=[
            pl.BlockSpec((1, 1, 1, nc, cw), lambda s, i, j: (s, i, j, 0, 0)),
            pl.BlockSpec((1, 8, 2 * cw), lambda s, i, j: (s, 0, 0)),
            pl.BlockSpec((1, 2 * cw, PHI_HIDDEN), lambda s, i, j: (s, 0, 0)),
            pl.BlockSpec((1, 1, PHI_HIDDEN), lambda s, i, j: (s, 0, 0)),
            pl.BlockSpec((1, PHI_HIDDEN, HEAD_DIM), lambda s, i, j: (s, 0, 0)),
        ],
        out_specs=pl.BlockSpec((1, 1, 1, nc, HEAD_DIM), lambda s, i, j: (s, i, j, 0, 0)),
        compiler_params=_cparams(("parallel", "parallel", "parallel")),
        name="nsa_compress",
    )(chunks, pe_flat8, w1, b1, w2)


def _softmax_pv(s, vT_tile, m_ref, acc_ref):
    m_old = m_ref[...]
    m_new = jnp.maximum(m_old, jnp.max(s, axis=0, keepdims=True))
    alpha = jnp.exp2(m_old - m_new)
    p = jnp.exp2(s - m_new).astype(jnp.bfloat16)
    acc_ref[...] = alpha * acc_ref[...] + _dot(vT_tile, p)
    m_ref[...] = m_new


def _nsa_attn_kernel(qT_ref, qrT_ref, kc_ref, vcT_ref, ovT_ref, kaug_ref, vsT_ref, kw_ref, vwT_ref, tri_ref,
                     gate_ref, o_ref, m_s, acc_s, s_buf, *, n_sel, top_n):
    g_heads = NSA_GROUP
    tq = ATT_T
    cols = g_heads * tq
    i = pl.program_id(2)
    qs = i * tq
    qT = jnp.concatenate([qT_ref[0, g] for g in range(g_heads)], axis=1)
    qrT = jnp.concatenate([qrT_ref[0, g] for g in range(g_heads)], axis=1)
    t_col = qs + (lax.broadcasted_iota(jnp.int32, (1, cols), 1) & (tq - 1))
    causal_bias = jnp.concatenate([tri_ref[0]] * g_heads, axis=1)
    lower_bias = jnp.concatenate([tri_ref[1]] * g_heads, axis=1)

    kc = kc_ref[0, 0]
    nc = kc.shape[0]
    sc = _dot(kc, qT)
    i0 = jnp.maximum(i - 2, 0)
    i1 = jnp.maximum(i - 1, 0)
    s0 = _dot(kw_ref[0, 0, i0], qrT) + lower_bias
    s1 = _dot(kw_ref[0, 0, i1], qrT)
    s2 = _dot(kw_ref[0, 0, i], qrT) + causal_bias

    c_end = lax.broadcasted_iota(jnp.int32, (nc, 1), 0) * CMP_STRIDE + (CMP_BLOCK - 1)
    sc = jnp.where(c_end <= t_col, sc, NEG)
    mc = jnp.maximum(jnp.max(sc, axis=0, keepdims=True), NEG_FLOOR)
    ec = jnp.exp2(sc - mc)
    lc = jnp.sum(ec, axis=0, keepdims=True)
    pc = ec * jnp.where(lc > 0.0, 1.0 / lc, 0.0)
    o_c = _dot(vcT_ref[0, 0], pc.astype(jnp.bfloat16))

    psum = pc[:, 0:tq]
    for g in range(1, g_heads):
        psum = psum + pc[:, g * tq:(g + 1) * tq]
    p_hi = psum.astype(jnp.bfloat16)
    p_lo = (psum - p_hi.astype(jnp.float32)).astype(jnp.bfloat16)
    ovT = ovT_ref[...]
    imp = _dot(ovT, p_hi) + _dot(ovT, p_lo)

    mw = jnp.maximum(jnp.maximum(s0.max(axis=0, keepdims=True), s1.max(axis=0, keepdims=True)),
                     s2.max(axis=0, keepdims=True))
    r0 = _dot(vwT_ref[0, 0, i0], jnp.exp2(s0 - mw).astype(jnp.bfloat16)) * jnp.where(i >= 2, 1.0, 0.0)
    r1 = _dot(vwT_ref[0, 0, i1], jnp.exp2(s1 - mw).astype(jnp.bfloat16)) * jnp.where(i >= 1, 1.0, 0.0)
    r2 = _dot(vwT_ref[0, 0, i], jnp.exp2(s2 - mw).astype(jnp.bfloat16))
    accw = r0 + r1 + r2
    o_w = accw[:HEAD_DIM] * (1.0 / accw[HEAD_DIM:HEAD_DIM + 1])

    blk = lax.broadcasted_iota(jnp.int32, (n_sel, tq), 0)
    blk_f = blk.astype(jnp.float32)
    cur = (qs + lax.broadcasted_iota(jnp.int32, (1, tq), 1)) >> int(math.log2(SEL_BLOCK))
    work = jnp.where(blk == 0, NEG, jnp.where(blk >= cur - 1, NEG, imp))
    for _ in range(top_n - N_FORCED):
        top = jnp.max(work, axis=0, keepdims=True)
        first = jnp.min(jnp.where(work == top, blk_f, float(n_sel)), axis=0, keepdims=True)
        work = jnp.where(blk_f == first, REMOVED, work)
    picked = jnp.where(work == REMOVED, 0.0, MASK_BIAS)
    picked = jnp.where(blk == 0, 0.0, jnp.where(blk >= cur - 1, 0.0, picked))
    selbias = jnp.where(blk <= cur, picked, MASK_BIAS).astype(jnp.bfloat16)
    q_aug = jnp.concatenate(
        [qrT, jnp.zeros((LANES - HEAD_DIM, cols), jnp.bfloat16),
         jnp.concatenate([selbias] * g_heads, axis=1)], axis=0)

    m_s[...] = jnp.full(m_s.shape, NEG, jnp.float32)
    acc_s[...] = jnp.zeros(acc_s.shape, jnp.float32)

    s_buf[...] = _dot(kaug_ref[0, 0, 0], q_aug)

    def body(kt, carry):
        s_cur = s_buf[...]
        s_buf[...] = _dot(kaug_ref[0, 0, kt + 1], q_aug)
        _softmax_pv(s_cur, vsT_ref[0, 0, kt], m_s, acc_s)
        return carry

    lax.fori_loop(0, i, body, 0)
    _softmax_pv(s_buf[...] + causal_bias, vsT_ref[0, 0, i], m_s, acc_s)
    acc = acc_s[...]
    o_s = acc[:HEAD_DIM] * (1.0 / acc[HEAD_DIM:HEAD_DIM + 1])

    gate = gate_ref[0, 0, 0]
    out = gate[0:1] * o_c + gate[1:2] * o_s + gate[2:3] * o_w
    o_ref[0, 0, 0] = out.astype(o_ref.dtype)


def nsa_attention(qT, qrT, kcmp, vcmpT, ovT, kaug, vsT, kw, vwT, tri, gates, seq):
    b = qT.shape[0]
    hk = NSA_KV_HEADS
    nq = seq // ATT_T
    n_sel = seq // SEL_BLOCK
    top_n = min(SEL_TOP_N, n_sel)
    assert top_n > N_FORCED
    nc = kcmp.shape[2]
    cols = NSA_GROUP * ATT_T
    kd = kaug.shape[-1]
    dv = HEAD_DIM + V_AUG
    kern = functools.partial(_nsa_attn_kernel, n_sel=n_sel, top_n=top_n)
    return pl.pallas_call(
        kern,
        out_shape=jax.ShapeDtypeStruct((b, hk, nq, HEAD_DIM, cols), jnp.bfloat16),
        grid=(b, hk, nq),
        in_specs=[
            pl.BlockSpec((1, NSA_GROUP, HEAD_DIM, ATT_T), lambda bi, h, i: (bi, h, 0, i)),
            pl.BlockSpec((1, NSA_GROUP, HEAD_DIM, ATT_T), lambda bi, h, i: (bi, h, 0, i)),
            pl.BlockSpec((1, 1, nc, HEAD_DIM), lambda bi, h, i: (bi, h, 0, 0)),
            pl.BlockSpec((1, 1, HEAD_DIM, nc), lambda bi, h, i: (bi, h, 0, 0)),
            pl.BlockSpec((n_sel, nc), lambda bi, h, i: (0, 0)),
            pl.BlockSpec((1, 1, nq, ATT_T, kd), lambda bi, h, i: (bi, h, 0, 0, 0)),
            pl.BlockSpec((1, 1, nq, dv, ATT_T), lambda bi, h, i: (bi, h, 0, 0, 0)),
            pl.BlockSpec((1, 1, nq, ATT_T, HEAD_DIM), lambda bi, h, i: (bi, h, 0, 0, 0)),
            pl.BlockSpec((1, 1, nq, dv, ATT_T), lambda bi, h, i: (bi, h, 0, 0, 0)),
            pl.BlockSpec((2, ATT_T, ATT_T), lambda bi, h, i: (0, 0, 0)),
            pl.BlockSpec((1, 1, 1, 3, cols), lambda bi, h, i: (bi, h, i, 0, 0)),
        ],
        out_specs=pl.BlockSpec((1, 1, 1, HEAD_DIM, cols), lambda bi, h, i: (bi, h, i, 0, 0)),
        scratch_shapes=[
            pltpu.VMEM((1, cols), jnp.float32),
            pltpu.VMEM((dv, cols), jnp.float32),
            pltpu.VMEM((ATT_T, cols), jnp.float32),
        ],
        compiler_params=_cparams(("parallel", "parallel", "arbitrary")),
        name="nsa_attention",
    )(qT, qrT, kcmp, vcmpT, ovT, kaug, vsT, kw, vwT, tri, gates)


def _diff_attn_kernel(qT_ref, k_ref, vT_ref, tri_ref, lam_ref, g_ref, o_ref, m_s, acc_s, s_buf, *, lam_init):
    tq = ATT_T
    i = pl.program_id(2)
    zero = jnp.zeros((DIFF_QK_DIM, tq), jnp.bfloat16)
    q_bd = []
    for hh in range(DIFF_HPS):
        q2 = qT_ref[0, hh]
        q_bd.append(jnp.concatenate(
            [jnp.concatenate([q2[:DIFF_QK_DIM], zero], axis=1),
             jnp.concatenate([zero, q2[DIFF_QK_DIM:]], axis=1)], axis=0))
    causal_bias = jnp.concatenate([tri_ref[0]] * 2, axis=1)

    m_s[...] = jnp.full(m_s.shape, NEG, jnp.float32)
    acc_s[...] = jnp.zeros(acc_s.shape, jnp.float32)

    for hh in range(DIFF_HPS):
        s_buf[hh] = _dot(k_ref[0, hh, 0], q_bd[hh])

    def body(kt, carry):
        s_cur = [s_buf[hh] for hh in range(DIFF_HPS)]
        for hh in range(DIFF_HPS):
            s_buf[hh] = _dot(k_ref[0, hh, kt + 1], q_bd[hh])
        for hh in range(DIFF_HPS):
            _softmax_pv(s_cur[hh], vT_ref[0, hh, kt], m_s.at[hh], acc_s.at[hh])
        return carry

    lax.fori_loop(0, i, body, 0)
    for hh in range(DIFF_HPS):
        _softmax_pv(s_buf[hh] + causal_bias, vT_ref[0, hh, i], m_s.at[hh], acc_s.at[hh])

    lv = lam_ref[...]
    lam = (jnp.exp(jnp.sum(lv[0:1] * lv[1:2], axis=1, keepdims=True))
           - jnp.exp(jnp.sum(lv[2:3] * lv[3:4], axis=1, keepdims=True)) + lam_init)
    g = jnp.concatenate([g_ref[...]] * (tq // LANES), axis=1)
    for hh in range(DIFF_HPS):
        acc = acc_s[hh]
        o = acc[:DIFF_V_DIM] * (1.0 / acc[DIFF_V_DIM:DIFF_V_DIM + 1])
        a = o[:, :tq] - lam * o[:, tq:]
        y = a * lax.rsqrt(jnp.mean(a * a, axis=0, keepdims=True) + EPS)
        y = (y * g) * (1.0 - lam_init)
        o_ref[0, hh, 0] = y.astype(o_ref.dtype)


def diff_attention(qT, kcat, vT, tri, lam_vec, g_rows, lam_init, seq):
    b = qT.shape[0]
    nq = seq // ATT_T
    cols = 2 * ATT_T
    dv = DIFF_V_DIM + V_AUG
    hps = DIFF_HPS
    return pl.pallas_call(
        functools.partial(_diff_attn_kernel, lam_init=lam_init),
        out_shape=jax.ShapeDtypeStruct((b, DIFF_HEADS, nq, DIFF_V_DIM, ATT_T), jnp.bfloat16),
        grid=(b, DIFF_HEADS // hps, nq),
        in_specs=[
            pl.BlockSpec((1, hps, 2 * DIFF_QK_DIM, ATT_T), lambda bi, h, i: (bi, h, 0, i)),
            pl.BlockSpec((1, hps, nq, ATT_T, 2 * DIFF_QK_DIM), lambda bi, h, i: (bi, h, 0, 0, 0)),
            pl.BlockSpec((1, hps, nq, dv, ATT_T), lambda bi, h, i: (bi, h, 0, 0, 0)),
            pl.BlockSpec((2, ATT_T, ATT_T), lambda bi, h, i: (0, 0, 0)),
            pl.BlockSpec((4, DIFF_QK_DIM), lambda bi, h, i: (0, 0)),
            pl.BlockSpec((DIFF_V_DIM, LANES), lambda bi, h, i: (0, 0)),
        ],
        out_specs=pl.BlockSpec((1, hps, 1, DIFF_V_DIM, ATT_T), lambda bi, h, i: (bi, h, i, 0, 0)),
        scratch_shapes=[
            pltpu.VMEM((hps, 1, cols), jnp.float32),
            pltpu.VMEM((hps, dv, cols), jnp.float32),
            pltpu.VMEM((hps, ATT_T, cols), jnp.float32),
        ],
        compiler_params=_cparams(("parallel", "parallel", "arbitrary")),
        name="diff_attention",
    )(qT, kcat, vT, tri, lam_vec, g_rows)


HALO = SUBLANES


def _ffn_kernel(x_ref, xh_ref, g_ref, wg_ref, wv_ref, cwg_ref, cwv_ref, cbg_ref, cbv_ref, wo_ref, fg_ref,
                o_ref, h_s, acc_s, *, tiles_per_seq, final_norm):
    i = pl.program_id(0)
    f = pl.program_id(1)

    @pl.when(f == 0)
    def _():
        g = g_ref[...]
        h_s[HALO:, :] = _rms(x_ref[...], g).astype(jnp.bfloat16)
        hh = _rms(xh_ref[...], g)
        hh = jnp.where(i % tiles_per_seq == 0, 0.0, hh)
        h_s[:HALO, :] = hh.astype(jnp.bfloat16)
        acc_s[...] = jnp.zeros(acc_s.shape, jnp.float32)

    h = h_s[...]

    def conv(u, cw_ref, cb_ref):
        cw = cw_ref[...]
        c = cb_ref[...] + u[HALO:] * cw[2:3]
        c = c + pltpu.roll(u, 1, 0)[HALO:] * cw[1:2]
        c = c + pltpu.roll(u, 2, 0)[HALO:] * cw[0:1]
        return c

    gate = conv(_dot(h, wg_ref[...]), cwg_ref, cbg_ref)
    val = conv(_dot(h, wv_ref[...]), cwv_ref, cbv_ref)
    act = (gate * jax.nn.sigmoid(gate)) * val
    acc_s[...] += _dot(act.astype(jnp.bfloat16), wo_ref[...])

    @pl.when(f == pl.num_programs(1) - 1)
    def _():
        y = x_ref[...] + acc_s[...]
        if final_norm:
            y = _rms(y, fg_ref[...])
        o_ref[...] = y


def ffn(x2d, g, wg, wv, cwg, cwv, cbg, cbv, wo, final_g, seq):
    m, d = x2d.shape
    tm = ROW_TILE
    nf = FFN_PAD // FFN_CHUNK
    tiles_per_seq = seq // tm
    final_norm = final_g is not None
    fg = (final_g if final_norm else g).reshape(1, d)
    kern = functools.partial(_ffn_kernel, tiles_per_seq=tiles_per_seq, final_norm=final_norm)
    return pl.pallas_call(
        kern,
        out_shape=jax.ShapeDtypeStruct((m, d), jnp.float32),
        grid=(m // tm, nf),
        in_specs=[
            pl.BlockSpec((tm, d), lambda i, f: (i, 0)),
            pl.BlockSpec((HALO, d), lambda i, f: (jnp.maximum(i * (tm // HALO) - 1, 0), 0)),
            pl.BlockSpec((1, d), lambda i, f: (0, 0)),
            pl.BlockSpec((d, FFN_CHUNK), lambda i, f: (0, f)),
            pl.BlockSpec((d, FFN_CHUNK), lambda i, f: (0, f)),
            pl.BlockSpec((CONV_WIDTH, FFN_CHUNK), lambda i, f: (0, f)),
            pl.BlockSpec((CONV_WIDTH, FFN_CHUNK), lambda i, f: (0, f)),
            pl.BlockSpec((1, FFN_CHUNK), lambda i, f: (0, f)),
            pl.BlockSpec((1, FFN_CHUNK), lambda i, f: (0, f)),
            pl.BlockSpec((FFN_CHUNK, d), lambda i, f: (f, 0)),
            pl.BlockSpec((1, d), lambda i, f: (0, 0)),
        ],
        out_specs=pl.BlockSpec((tm, d), lambda i, f: (i, 0)),
        scratch_shapes=[
            pltpu.VMEM((HALO + tm, d), jnp.bfloat16),
            pltpu.VMEM((tm, d), jnp.float32),
        ],
        compiler_params=_cparams(("parallel", "arbitrary")),
        name="conv_glu_ffn",
    )(x2d, x2d, g.reshape(1, d), wg, wv, cwg, cwv, cbg, cbv, wo, fg)


def _rope_tables(seq):
    half = ROT_DIM // 2
    inv = ROPE_THETA ** (-jnp.arange(half, dtype=jnp.float32) / half)
    ang = jnp.arange(seq).astype(jnp.float32)[:, None] * inv[None, :]
    cos, sin = jnp.cos(ang), jnp.sin(ang)
    ones = jnp.ones((seq, HEAD_DIM - ROT_DIM), jnp.float32)
    zeros = jnp.zeros((seq, HEAD_DIM - ROT_DIM), jnp.float32)
    z8 = jnp.zeros((seq, half), jnp.float32)
    c = jnp.concatenate([cos, cos, ones], axis=1)
    s_lo = jnp.concatenate([-sin, z8, zeros], axis=1)
    s_hi = jnp.concatenate([z8, sin, zeros], axis=1)
    tab = jnp.stack([c, s_lo, s_hi])
    return jnp.concatenate([tab] * (LANES // HEAD_DIM), axis=2)


def _triangle_biases():
    rk = jnp.arange(ATT_T)[:, None]
    rq = jnp.arange(ATT_T)[None, :]
    causal = jnp.where(rk <= rq, 0.0, MASK_BIAS)
    lower = jnp.where(rk > rq, 0.0, MASK_BIAS)
    return jnp.stack([causal, lower]).astype(jnp.float32)


def _pad_cols(w, n):
    return jnp.pad(w, ((0, 0), (0, n - w.shape[1])))


def _vT_aug(v, bsz, seq, heads, dv):
    nt = seq // ATT_T
    vT = v.reshape(bsz, nt, ATT_T, heads, dv).transpose(0, 3, 1, 4, 2)
    extra = jnp.zeros((bsz, heads, nt, V_AUG, ATT_T), v.dtype).at[:, :, :, 0, :].set(1.0)
    return jnp.concatenate([vT, extra], axis=3)


def _nsa_layer(x2d, bsz, seq, g, w_in, w_out, pe, w1, b1, w2, rope_tab, tri):
    bf = jnp.bfloat16
    hk, hd = NSA_KV_HEADS, HEAD_DIM
    wq = w_in[:, :NSA_Q_W]
    wkv = [w_in[:, NSA_Q_W + k * NSA_KV_W:NSA_Q_W + (k + 1) * NSA_KV_W] for k in range(6)]
    wg = _pad_cols(w_in[:, NSA_Q_W + 6 * NSA_KV_W:], COL_TILE)
    w_all = jnp.concatenate([wq, wq] + wkv + [wg], axis=1).astype(bf)
    nq_t = NSA_Q_W // COL_TILE
    modes = ([MODE_SCALE] * nq_t + [MODE_ROPE_SCALE] * nq_t
             + [MODE_PLAIN, MODE_PLAIN, MODE_ROPE, MODE_PLAIN, MODE_ROPE, MODE_PLAIN, MODE_SIGMOID])
    proj = norm_linear(x2d, g, w_all, modes, rope_tab, seq).reshape(bsz, seq, -1)

    off = 0
    q = proj[..., off:off + NSA_Q_W]; off += NSA_Q_W
    qr = proj[..., off:off + NSA_Q_W]; off += NSA_Q_W
    kc, vc, ks, vs, kw, vw = (proj[..., off + k * NSA_KV_W:off + (k + 1) * NSA_KV_W] for k in range(6))
    off += 6 * NSA_KV_W
    gates = proj[..., off:off + 3 * NSA_HEADS]

    qT = q.reshape(bsz, seq, NSA_HEADS, hd).transpose(0, 2, 3, 1)
    qrT = qr.reshape(bsz, seq, NSA_HEADS, hd).transpose(0, 2, 3, 1)

    nch = seq // CMP_STRIDE
    def chunked(a):
        return a.reshape(bsz, nch, CMP_STRIDE, hk, hd).transpose(0, 3, 1, 2, 4).reshape(bsz, hk, nch, CMP_STRIDE * hd)
    chunks = jnp.stack([chunked(kc), chunked(vc)])
    pe_flat8 = jnp.broadcast_to(pe.reshape(2, 1, CMP_BLOCK * hd), (2, 8, CMP_BLOCK * hd)).astype(bf)
    cmp = compress(chunks, pe_flat8, w1.astype(bf), b1.reshape(2, 1, PHI_HIDDEN), w2.astype(bf))
    kcmp = cmp[0]
    vcmpT = cmp[1].transpose(0, 1, 3, 2)

    n_sel = seq // SEL_BLOCK
    ci = jnp.arange(nch)[None, :]
    sj = jnp.arange(n_sel)[:, None]
    ovT = ((ci * CMP_STRIDE < (sj + 1) * SEL_BLOCK) & (ci * CMP_STRIDE + CMP_BLOCK > sj * SEL_BLOCK)
           & (ci < nch - 1)).astype(bf)

    nt = seq // ATT_T
    ks_h = ks.reshape(bsz, seq, hk, hd).transpose(0, 2, 1, 3)
    onehot = (jnp.arange(seq)[:, None] // SEL_BLOCK == jnp.arange(n_sel)[None, :]).astype(bf)
    kaug = jnp.concatenate(
        [ks_h, jnp.zeros((bsz, hk, seq, LANES - hd), bf),
         jnp.broadcast_to(onehot, (bsz, hk, seq, n_sel))], axis=-1)
    kaug = kaug.reshape(bsz, hk, nt, ATT_T, LANES + n_sel)
    vsT = _vT_aug(vs, bsz, seq, hk, hd)
    kw_t = kw.reshape(bsz, nt, ATT_T, hk, hd).transpose(0, 3, 1, 2, 4)
    vwT = _vT_aug(vw, bsz, seq, hk, hd)
    gates_t = gates.reshape(bsz, nt, ATT_T, hk, NSA_GROUP, 3).transpose(0, 3, 1, 5, 4, 2)
    gates_t = gates_t.reshape(bsz, hk, nt, 3, NSA_GROUP * ATT_T).astype(jnp.float32)

    o = nsa_attention(qT, qrT, kcmp, vcmpT, ovT, kaug, vsT, kw_t, vwT, tri, gates_t, seq)
    o = o.reshape(bsz, hk, nt, hd, NSA_GROUP, ATT_T).transpose(0, 2, 5, 1, 4, 3)
    o = o.reshape(bsz * seq, NSA_Q_W)
    return res_linear(x2d, o, w_out.astype(bf))


def _shared_kv(x2d, bsz, seq, g, kv_w, rope_tab):
    nk = DIFF_Q_W // COL_TILE
    nv = DIFF_V_W // COL_TILE
    kv = norm_linear(x2d, g, kv_w.astype(jnp.bfloat16), [MODE_ROPE] * nk + [MODE_PLAIN] * nv, rope_tab, seq)
    kv = kv.reshape(bsz, seq, -1)
    k = kv[..., :DIFF_Q_W].reshape(bsz, seq // ATT_T, ATT_T, DIFF_HEADS, 2 * DIFF_QK_DIM)
    kcat = k.transpose(0, 3, 1, 2, 4)
    vT = _vT_aug(kv[..., DIFF_Q_W:], bsz, seq, DIFF_HEADS, DIFF_V_DIM)
    return kcat, vT


def _diff_layer(x2d, bsz, seq, g, w_q, lam_vec, subln_g, w_out, kcat, vT, lam_init, rope_tab, tri):
    bf = jnp.bfloat16
    q = norm_linear(x2d, g, w_q.astype(bf), [MODE_ROPE_SCALE] * (DIFF_Q_W // COL_TILE), rope_tab, seq)
    qT = q.reshape(bsz, seq, DIFF_HEADS, 2 * DIFF_QK_DIM).transpose(0, 2, 3, 1)
    g_rows = jnp.broadcast_to(subln_g.astype(jnp.float32)[:, None], (DIFF_V_DIM, LANES))
    o = diff_attention(qT, kcat, vT, tri, lam_vec.astype(jnp.float32), g_rows, lam_init, seq)
    o = o.transpose(0, 2, 4, 1, 3).reshape(bsz * seq, DIFF_V_W)
    return res_linear(x2d, o, w_out.astype(bf))


def _ffn_layer(x2d, seq, g, w_in, conv_w, conv_b, w_out, final_g):
    bf = jnp.bfloat16
    f = FFN_DIM
    wg = _pad_cols(w_in[:, :f], FFN_PAD).astype(bf)
    wv = _pad_cols(w_in[:, f:], FFN_PAD).astype(bf)
    cwg = _pad_cols(conv_w[:, :f], FFN_PAD)
    cwv = _pad_cols(conv_w[:, f:], FFN_PAD)
    cbg = _pad_cols(conv_b[None, :f], FFN_PAD)
    cbv = _pad_cols(conv_b[None, f:], FFN_PAD)
    wo = jnp.pad(w_out, ((0, FFN_PAD - f), (0, 0))).astype(bf)
    return ffn(x2d, g, wg, wv, cwg, cwv, cbg, cbv, wo, final_g, seq)


def kernel(x, attn_norm_g, ffn_norm_g, nsa_w_in, nsa_w_out, nsa_phi_pe, nsa_phi_w1, nsa_phi_b1, nsa_phi_w2,
           kv_norm_g, kv_w, diff_w_q, diff_lambda, diff_subln_g, diff_w_out, ffn_w_in, ffn_conv_w, ffn_conv_b,
           ffn_w_out, final_norm_g):
    bsz, seq, d = x.shape
    x2d = x.reshape(bsz * seq, d)
    rope_tab = _rope_tables(seq)
    tri = _triangle_biases()
    kcat = vT = None
    for l in range(DEPTH):
        if l < N_A_LAYERS:
            x2d = _nsa_layer(x2d, bsz, seq, attn_norm_g[l], nsa_w_in[l], nsa_w_out[l], nsa_phi_pe[l],
                             nsa_phi_w1[l], nsa_phi_b1[l], nsa_phi_w2[l], rope_tab, tri)
        else:
            j = l - N_A_LAYERS
            lam_init = 0.8 - 0.6 * math.exp(-0.3 * l)
            x2d = _diff_layer(x2d, bsz, seq, attn_norm_g[l], diff_w_q[j], diff_lambda[j], diff_subln_g[j],
                              diff_w_out[j], kcat, vT, lam_init, rope_tab, tri)
        final_g = final_norm_g if l == DEPTH - 1 else None
        x2d = _ffn_layer(x2d, seq, ffn_norm_g[l], ffn_w_in[l], ffn_conv_w[l], ffn_conv_b[l], ffn_w_out[l], final_g)
        if l == N_A_LAYERS - 1:
            kcat, vT = _shared_kv(x2d, bsz, seq, kv_norm_g, kv_w, rope_tab)
    return x2d.reshape(bsz, seq, d)
```

```python
import functools
import math

import jax
import jax.numpy as jnp
from jax import lax
from jax.experimental import pallas as pl
from jax.experimental.pallas import tpu as pltpu

D_MODEL = 1024
DEPTH = 4
N_A_LAYERS = DEPTH // 2
NSA_HEADS = 16
NSA_KV_HEADS = 4
NSA_GROUP = NSA_HEADS // NSA_KV_HEADS
HEAD_DIM = 64
CMP_BLOCK = 32
CMP_STRIDE = 16
SEL_BLOCK = 64
SEL_TOP_N = 16
N_FORCED = 3
WINDOW = 512
PHI_HIDDEN = 256
NSA_Q_W = NSA_HEADS * HEAD_DIM
NSA_KV_W = NSA_KV_HEADS * HEAD_DIM
DIFF_HEADS = 8
DIFF_QK_DIM = 64
DIFF_V_DIM = 128
DIFF_Q_W = DIFF_HEADS * 2 * DIFF_QK_DIM
DIFF_V_W = DIFF_HEADS * DIFF_V_DIM
ROT_DIM = HEAD_DIM // 4
ROPE_THETA = 500000.0
FFN_DIM = 2752
CONV_WIDTH = 3
EPS = 1e-6
NEG = -1e30
FORCE = 1e4

LANES = 128
SUBLANES = 8
MXU_DIM = 256
VMEM_LIMIT_BYTES = 56 * 1024 * 1024

COL_TILE = MXU_DIM
ROW_TILE = 512
ATT_T = 256
DIFF_HPS = 4
NSA_HPS = 2
V_AUG = SUBLANES
FFN_PAD = 2816
FFN_CHUNK = FFN_PAD // 2
MASK_BIAS = -1e9
NEG_FLOOR = -1e20
REMOVED = -3.0e38

MODE_PLAIN, MODE_SCALE, MODE_ROPE, MODE_ROPE_SCALE, MODE_SIGMOID = range(5)
LOG2E = math.log2(math.e)
QK_SCALE = HEAD_DIM ** -0.5 * LOG2E


def _cparams(sem):
    return pltpu.CompilerParams(dimension_semantics=sem, vmem_limit_bytes=VMEM_LIMIT_BYTES)


def _rms(x, g):
    y = x * lax.rsqrt(jnp.mean(x * x, axis=-1, keepdims=True) + EPS)
    return y * g


def _dot(a, b):
    return jnp.dot(a, b, preferred_element_type=jnp.float32)


def _norm_linear_kernel(x_ref, g_ref, w_ref, rope_ref, o_ref, *, modes):
    h = _rms(x_ref[...], g_ref[...]).astype(jnp.bfloat16)
    for j, mode in enumerate(modes):
        c0 = j * COL_TILE
        y = _dot(h, w_ref[:, c0:c0 + COL_TILE])
        if mode in (MODE_ROPE, MODE_ROPE_SCALE):
            cos, s_lo, s_hi = rope_ref[0], rope_ref[1], rope_ref[2]
            parts = []
            for c in range(COL_TILE // LANES):
                yh = y[:, c * LANES:(c + 1) * LANES]
                up = pltpu.roll(yh, LANES - ROT_DIM // 2, 1)
                dn = pltpu.roll(yh, ROT_DIM // 2, 1)
                parts.append(yh * cos + up * s_lo + dn * s_hi)
            y = jnp.concatenate(parts, axis=1)
        if mode in (MODE_SCALE, MODE_ROPE_SCALE):
            y = y * QK_SCALE
        if mode == MODE_SIGMOID:
            y = jax.nn.sigmoid(y)
        o_ref[:, c0:c0 + COL_TILE] = y.astype(o_ref.dtype)


def norm_linear(x2d, g, w, modes, rope_tab, seq):
    m, d = x2d.shape
    n = w.shape[1]
    assert n == len(modes) * COL_TILE and m % ROW_TILE == 0 and seq % ROW_TILE == 0
    tiles_per_seq = seq // ROW_TILE
    return pl.pallas_call(
        functools.partial(_norm_linear_kernel, modes=tuple(modes)),
        out_shape=jax.ShapeDtypeStruct((m, n), jnp.bfloat16),
        grid=(m // ROW_TILE,),
        in_specs=[
            pl.BlockSpec((ROW_TILE, d), lambda i: (i, 0)),
            pl.BlockSpec((1, d), lambda i: (0, 0)),
            pl.BlockSpec((d, n), lambda i: (0, 0)),
            pl.BlockSpec((3, ROW_TILE, LANES), lambda i: (0, i % tiles_per_seq, 0)),
        ],
        out_specs=pl.BlockSpec((ROW_TILE, n), lambda i: (i, 0)),
        compiler_params=_cparams(("parallel",)),
        name="norm_linear",
    )(x2d, g.reshape(1, d), w, rope_tab)


def _res_linear_kernel(x_ref, a_ref, w_ref, o_ref):
    o_ref[...] = x_ref[...] + _dot(a_ref[...], w_ref[...])


def res_linear(x2d, a2d, w):
    m, d = x2d.shape
    k = a2d.shape[1]
    return pl.pallas_call(
        _res_linear_kernel,
        out_shape=jax.ShapeDtypeStruct((m, d), jnp.float32),
        grid=(m // ROW_TILE,),
        in_specs=[
            pl.BlockSpec((ROW_TILE, d), lambda i: (i, 0)),
            pl.BlockSpec((ROW_TILE, k), lambda i: (i, 0)),
            pl.BlockSpec((k, d), lambda i: (0, 0)),
        ],
        out_specs=pl.BlockSpec((ROW_TILE, d), lambda i: (i, 0)),
        compiler_params=_cparams(("parallel",)),
        name="res_linear",
    )(x2d, a2d, w)


def _compress_kernel(ch_ref, pe_ref, w1_ref, b1_ref, w2_ref, o_ref):
    ch = ch_ref[0, 0, 0]
    w1 = w1_ref[0]
    half = CMP_STRIDE * HEAD_DIM
    nc = ch.shape[0]
    first = _dot(ch, w1[:half])
    second = _dot(ch, w1[half:])
    second = pltpu.roll(second, nc - 1, 0)
    bias = _dot(pe_ref[0], w1)[0:1] + b1_ref[0]
    pre = first + second + bias
    hid = jax.nn.gelu(pre, approximate=True)
    o_ref[0, 0, 0] = _dot(hid.astype(jnp.bfloat16), w2_ref[0]).astype(o_ref.dtype)


def compress(chunks, pe_flat8, w1, b1, w2):
    _, b, hk, nc, cw = chunks.shape
    return pl.pallas_call(
        _compress_kernel,
        out_shape=jax.ShapeDtypeStruct((2, b, hk, nc, HEAD_DIM), jnp.bfloat16),
        grid=(2, b, hk),
        in_specs=[
            pl.BlockSpec((1, 1, 1, nc, cw), lambda s, i, j: (s, i, j, 0, 0)),
            pl.BlockSpec((1, 8, 2 * cw), lambda s, i, j: (s, 0, 0)),
            pl.BlockSpec((1, 2 * cw, PHI_HIDDEN), lambda s, i, j: (s, 0, 0)),
            pl.BlockSpec((1, 1, PHI_HIDDEN), lambda s, i, j: (s, 0, 0)),
            pl.BlockSpec((1, PHI_HIDDEN, HEAD_DIM), lambda s, i, j: (s, 0, 0)),
        ],
        out_specs=pl.BlockSpec((1, 1, 1, nc, HEAD_DIM), lambda s, i, j: (s, i, j, 0, 0)),
        compiler_params=_cparams(("parallel", "parallel", "parallel")),
        name="nsa_compress",
    )(chunks, pe_flat8, w1, b1, w2)


def _softmax_pv(s, vT_tile, m_ref, acc_ref):
    m_old = m_ref[...]
    m_new = jnp.maximum(m_old, jnp.max(s, axis=0, keepdims=True))
    alpha = jnp.exp2(m_old - m_new)
    p = jnp.exp2(s - m_new).astype(jnp.bfloat16)
    acc_ref[...] = alpha * acc_ref[...] + _dot(vT_tile, p)
    m_ref[...] = m_new


def _nsa_attn_kernel(qT_ref, qrT_ref, kc_ref, vcT_ref, ovT_ref, kaug_ref, vsT_ref, kw_ref, vwT_ref, tri_ref,
                     gate_ref, o_ref, m_s, acc_s, s_buf, *, n_sel, top_n):
    g_heads = NSA_GROUP
    tq = ATT_T
    cols = g_heads * tq
    i = pl.program_id(2)
    qs = i * tq
    heads = range(NSA_HPS)
    qT = [jnp.concatenate([qT_ref[0, hh * g_heads + g] for g in range(g_heads)], axis=1) for hh in heads]
    qrT = [jnp.concatenate([qrT_ref[0, hh * g_heads + g] for g in range(g_heads)], axis=1) for hh in heads]
    t_col = qs + (lax.broadcasted_iota(jnp.int32, (1, cols), 1) & (tq - 1))
    causal_bias = jnp.concatenate([tri_ref[0]] * g_heads, axis=1)
    lower_bias = jnp.concatenate([tri_ref[1]] * g_heads, axis=1)

    nc = kc_ref.shape[2]
    i0 = jnp.maximum(i - 2, 0)
    i1 = jnp.maximum(i - 1, 0)
    sc = [_dot(kc_ref[0, hh], qT[hh]) for hh in heads]
    sw = [(_dot(kw_ref[0, hh, i0], qrT[hh]) + lower_bias,
           _dot(kw_ref[0, hh, i1], qrT[hh]),
           _dot(kw_ref[0, hh, i], qrT[hh]) + causal_bias) for hh in heads]

    c_end = lax.broadcasted_iota(jnp.int32, (nc, 1), 0) * CMP_STRIDE + (CMP_BLOCK - 1)
    ovT = ovT_ref[...]
    o_c, imp = [], []
    for hh in heads:
        s = jnp.where(c_end <= t_col, sc[hh], NEG)
        mc = jnp.maximum(jnp.max(s, axis=0, keepdims=True), NEG_FLOOR)
        ec = jnp.exp2(s - mc)
        lc = jnp.sum(ec, axis=0, keepdims=True)
        pc = ec * jnp.where(lc > 0.0, 1.0 / lc, 0.0)
        o_c.append(_dot(vcT_ref[0, hh], pc.astype(jnp.bfloat16)))
        psum = pc[:, 0:tq]
        for g in range(1, g_heads):
            psum = psum + pc[:, g * tq:(g + 1) * tq]
        p_hi = psum.astype(jnp.bfloat16)
        p_lo = (psum - p_hi.astype(jnp.float32)).astype(jnp.bfloat16)
        imp.append(_dot(ovT, p_hi) + _dot(ovT, p_lo))

    o_w = []
    for hh in heads:
        s0, s1, s2 = sw[hh]
        mw = jnp.maximum(jnp.maximum(s0.max(axis=0, keepdims=True), s1.max(axis=0, keepdims=True)),
                         s2.max(axis=0, keepdims=True))
        r0 = _dot(vwT_ref[0, hh, i0], jnp.exp2(s0 - mw).astype(jnp.bfloat16)) * jnp.where(i >= 2, 1.0, 0.0)
        r1 = _dot(vwT_ref[0, hh, i1], jnp.exp2(s1 - mw).astype(jnp.bfloat16)) * jnp.where(i >= 1, 1.0, 0.0)
        r2 = _dot(vwT_ref[0, hh, i], jnp.exp2(s2 - mw).astype(jnp.bfloat16))
        accw = r0 + r1 + r2
        o_w.append(accw[:HEAD_DIM] * (1.0 / accw[HEAD_DIM:HEAD_DIM + 1]))

    blk = lax.broadcasted_iota(jnp.int32, (n_sel, tq), 0)
    blk_f = blk.astype(jnp.float32)
    cur = (qs + lax.broadcasted_iota(jnp.int32, (1, tq), 1)) >> int(math.log2(SEL_BLOCK))
    work = [jnp.where(blk == 0, NEG, jnp.where(blk >= cur - 1, NEG, imp[hh])) for hh in heads]
    for _ in range(top_n - N_FORCED):
        for hh in heads:
            top = jnp.max(work[hh], axis=0, keepdims=True)
            first = jnp.min(jnp.where(work[hh] == top, blk_f, float(n_sel)), axis=0, keepdims=True)
            work[hh] = jnp.where(blk_f == first, REMOVED, work[hh])
    q_aug = []
    for hh in heads:
        picked = jnp.where(work[hh] == REMOVED, 0.0, MASK_BIAS)
        picked = jnp.where(blk == 0, 0.0, jnp.where(blk >= cur - 1, 0.0, picked))
        selbias = jnp.where(blk <= cur, picked, MASK_BIAS).astype(jnp.bfloat16)
        q_aug.append(jnp.concatenate(
            [qrT[hh], jnp.zeros((LANES - HEAD_DIM, cols), jnp.bfloat16),
             jnp.concatenate([selbias] * g_heads, axis=1)], axis=0))

    m_s[...] = jnp.full(m_s.shape, NEG, jnp.float32)
    acc_s[...] = jnp.zeros(acc_s.shape, jnp.float32)

    for hh in heads:
        s_buf[hh] = _dot(kaug_ref[0, hh, 0], q_aug[hh])

    def body(kt, carry):
        s_cur = [s_buf[hh] for hh in heads]
        for hh in heads:
            s_buf[hh] = _dot(kaug_ref[0, hh, kt + 1], q_aug[hh])
        for hh in heads:
            _softmax_pv(s_cur[hh], vsT_ref[0, hh, kt], m_s.at[hh], acc_s.at[hh])
        return carry

    lax.fori_loop(0, i, body, 0)
    for hh in heads:
        _softmax_pv(s_buf[hh] + causal_bias, vsT_ref[0, hh, i], m_s.at[hh], acc_s.at[hh])
    for hh in heads:
        acc = acc_s[hh]
        o_s = acc[:HEAD_DIM] * (1.0 / acc[HEAD_DIM:HEAD_DIM + 1])
        gate = gate_ref[0, hh, 0]
        out = gate[0:1] * o_c[hh] + gate[1:2] * o_s + gate[2:3] * o_w[hh]
        o_ref[0, hh, 0] = out.astype(o_ref.dtype)


def nsa_attention(qT, qrT, kcmp, vcmpT, ovT, kaug, vsT, kw, vwT, tri, gates, seq):
    b = qT.shape[0]
    hk = NSA_KV_HEADS
    nq = seq // ATT_T
    n_sel = seq // SEL_BLOCK
    top_n = min(SEL_TOP_N, n_sel)
    assert top_n > N_FORCED
    nc = kcmp.shape[2]
    cols = NSA_GROUP * ATT_T
    kd = kaug.shape[-1]
    dv = HEAD_DIM + V_AUG
    hps = NSA_HPS
    resident = pl.Buffered(1)
    kern = functools.partial(_nsa_attn_kernel, n_sel=n_sel, top_n=top_n)
    return pl.pallas_call(
        kern,
        out_shape=jax.ShapeDtypeStruct((b, hk, nq, HEAD_DIM, cols), jnp.bfloat16),
        grid=(b, hk // hps, nq),
        in_specs=[
            pl.BlockSpec((1, hps * NSA_GROUP, HEAD_DIM, ATT_T), lambda bi, h, i: (bi, h, 0, i)),
            pl.BlockSpec((1, hps * NSA_GROUP, HEAD_DIM, ATT_T), lambda bi, h, i: (bi, h, 0, i)),
            pl.BlockSpec((1, hps, nc, HEAD_DIM), lambda bi, h, i: (bi, h, 0, 0)),
            pl.BlockSpec((1, hps, HEAD_DIM, nc), lambda bi, h, i: (bi, h, 0, 0)),
            pl.BlockSpec((n_sel, nc), lambda bi, h, i: (0, 0)),
            pl.BlockSpec((1, hps, nq, ATT_T, kd), lambda bi, h, i: (bi, h, 0, 0, 0), pipeline_mode=resident),
            pl.BlockSpec((1, hps, nq, dv, ATT_T), lambda bi, h, i: (bi, h, 0, 0, 0), pipeline_mode=resident),
            pl.BlockSpec((1, hps, nq, ATT_T, HEAD_DIM), lambda bi, h, i: (bi, h, 0, 0, 0), pipeline_mode=resident),
            pl.BlockSpec((1, hps, nq, dv, ATT_T), lambda bi, h, i: (bi, h, 0, 0, 0), pipeline_mode=resident),
            pl.BlockSpec((2, ATT_T, ATT_T), lambda bi, h, i: (0, 0, 0)),
            pl.BlockSpec((1, hps, 1, 3, cols), lambda bi, h, i: (bi, h, i, 0, 0)),
        ],
        out_specs=pl.BlockSpec((1, hps, 1, HEAD_DIM, cols), lambda bi, h, i: (bi, h, i, 0, 0)),
        scratch_shapes=[
            pltpu.VMEM((hps, 1, cols), jnp.float32),
            pltpu.VMEM((hps, dv, cols), jnp.float32),
            pltpu.VMEM((hps, ATT_T, cols), jnp.float32),
        ],
        compiler_params=_cparams(("parallel", "parallel", "arbitrary")),
        name="nsa_attention",
    )(qT, qrT, kcmp, vcmpT, ovT, kaug, vsT, kw, vwT, tri, gates)


def _diff_attn_kernel(qT_ref, k_ref, vT_ref, tri_ref, lam_ref, g_ref, o_ref, m_s, acc_s, s_buf, *, lam_init):
    tq = ATT_T
    i = pl.program_id(2)
    zero = jnp.zeros((DIFF_QK_DIM, tq), jnp.bfloat16)
    q_bd = []
    for hh in range(DIFF_HPS):
        q2 = qT_ref[0, hh]
        q_bd.append(jnp.concatenate(
            [jnp.concatenate([q2[:DIFF_QK_DIM], zero], axis=1),
             jnp.concatenate([zero, q2[DIFF_QK_DIM:]], axis=1)], axis=0))
    causal_bias = jnp.concatenate([tri_ref[0]] * 2, axis=1)

    m_s[...] = jnp.full(m_s.shape, NEG, jnp.float32)
    acc_s[...] = jnp.zeros(acc_s.shape, jnp.float32)

    for hh in range(DIFF_HPS):
        s_buf[hh] = _dot(k_ref[0, hh, 0], q_bd[hh])

    def body(kt, carry):
        s_cur = [s_buf[hh] for hh in range(DIFF_HPS)]
        for hh in range(DIFF_HPS):
            s_buf[hh] = _dot(k_ref[0, hh, kt + 1], q_bd[hh])
        for hh in range(DIFF_HPS):
            _softmax_pv(s_cur[hh], vT_ref[0, hh, kt], m_s.at[hh], acc_s.at[hh])
        return carry

    lax.fori_loop(0, i, body, 0)
    for hh in range(DIFF_HPS):
        _softmax_pv(s_buf[hh] + causal_bias, vT_ref[0, hh, i], m_s.at[hh], acc_s.at[hh])

    lv = lam_ref[...]
    lam = (jnp.exp(jnp.sum(lv[0:1] * lv[1:2], axis=1, keepdims=True))
           - jnp.exp(jnp.sum(lv[2:3] * lv[3:4], axis=1, keepdims=True)) + lam_init)
    g = jnp.concatenate([g_ref[...]] * (tq // LANES), axis=1)
    for hh in range(DIFF_HPS):
        acc = acc_s[hh]
        o = acc[:DIFF_V_DIM] * (1.0 / acc[DIFF_V_DIM:DIFF_V_DIM + 1])
        a = o[:, :tq] - lam * o[:, tq:]
        y = a * lax.rsqrt(jnp.mean(a * a, axis=0, keepdims=True) + EPS)
        y = (y * g) * (1.0 - lam_init)
        o_ref[0, hh, 0] = y.astype(o_ref.dtype)


def diff_attention(qT, kcat, vT, tri, lam_vec, g_rows, lam_init, seq):
    b = qT.shape[0]
    nq = seq // ATT_T
    cols = 2 * ATT_T
    dv = DIFF_V_DIM + V_AUG
    hps = DIFF_HPS
    return pl.pallas_call(
        functools.partial(_diff_attn_kernel, lam_init=lam_init),
        out_shape=jax.ShapeDtypeStruct((b, DIFF_HEADS, nq, DIFF_V_DIM, ATT_T), jnp.bfloat16),
        grid=(b, DIFF_HEADS // hps, nq),
        in_specs=[
            pl.BlockSpec((1, hps, 2 * DIFF_QK_DIM, ATT_T), lambda bi, h, i: (bi, h, 0, i)),
            pl.BlockSpec((1, hps, nq, ATT_T, 2 * DIFF_QK_DIM), lambda bi, h, i: (bi, h, 0, 0, 0)),
            pl.BlockSpec((1, hps, nq, dv, ATT_T), lambda bi, h, i: (bi, h, 0, 0, 0)),
            pl.BlockSpec((2, ATT_T, ATT_T), lambda bi, h, i: (0, 0, 0)),
            pl.BlockSpec((4, DIFF_QK_DIM), lambda bi, h, i: (0, 0)),
            pl.BlockSpec((DIFF_V_DIM, LANES), lambda bi, h, i: (0, 0)),
        ],
        out_specs=pl.BlockSpec((1, hps, 1, DIFF_V_DIM, ATT_T), lambda bi, h, i: (bi, h, i, 0, 0)),
        scratch_shapes=[
            pltpu.VMEM((hps, 1, cols), jnp.float32),
            pltpu.VMEM((hps, dv, cols), jnp.float32),
            pltpu.VMEM((hps, ATT_T, cols), jnp.float32),
        ],
        compiler_params=_cparams(("parallel", "parallel", "arbitrary")),
        name="diff_attention",
    )(qT, kcat, vT, tri, lam_vec, g_rows)


HALO = SUBLANES


def _ffn_kernel(x_ref, xh_ref, g_ref, wg_ref, wv_ref, cwg_ref, cwv_ref, cbg_ref, cbv_ref, wo_ref, fg_ref,
                o_ref, h_s, acc_s, *, tiles_per_seq, final_norm):
    i = pl.program_id(0)
    f = pl.program_id(1)

    @pl.when(f == 0)
    def _():
        g = g_ref[...]
        h_s[HALO:, :] = _rms(x_ref[...], g).astype(jnp.bfloat16)
        hh = _rms(xh_ref[...], g)
        hh = jnp.where(i % tiles_per_seq == 0, 0.0, hh)
        h_s[:HALO, :] = hh.astype(jnp.bfloat16)
        acc_s[...] = jnp.zeros(acc_s.shape, jnp.float32)

    h = h_s[...]

    def conv(u, cw_ref, cb_ref):
        cw = cw_ref[...]
        c = cb_ref[...] + u[HALO:] * cw[2:3]
        c = c + pltpu.roll(u, 1, 0)[HALO:] * cw[1:2]
        c = c + pltpu.roll(u, 2, 0)[HALO:] * cw[0:1]
        return c

    gate = conv(_dot(h, wg_ref[...]), cwg_ref, cbg_ref)
    val = conv(_dot(h, wv_ref[...]), cwv_ref, cbv_ref)
    act = (gate * jax.nn.sigmoid(gate)) * val
    acc_s[...] += _dot(act.astype(jnp.bfloat16), wo_ref[...])

    @pl.when(f == pl.num_programs(1) - 1)
    def _():
        y = x_ref[...] + acc_s[...]
        if final_norm:
            y = _rms(y, fg_ref[...])
        o_ref[...] = y


def ffn(x2d, g, wg, wv, cwg, cwv, cbg, cbv, wo, final_g, seq):
    m, d = x2d.shape
    tm = ROW_TILE
    nf = FFN_PAD // FFN_CHUNK
    tiles_per_seq = seq // tm
    final_norm = final_g is not None
    fg = (final_g if final_norm else g).reshape(1, d)
    kern = functools.partial(_ffn_kernel, tiles_per_seq=tiles_per_seq, final_norm=final_norm)
    return pl.pallas_call(
        kern,
        out_shape=jax.ShapeDtypeStruct((m, d), jnp.float32),
        grid=(m // tm, nf),
        in_specs=[
            pl.BlockSpec((tm, d), lambda i, f: (i, 0)),
            pl.BlockSpec((HALO, d), lambda i, f: (jnp.maximum(i * (tm // HALO) - 1, 0), 0)),
            pl.BlockSpec((1, d), lambda i, f: (0, 0)),
            pl.BlockSpec((d, FFN_CHUNK), lambda i, f: (0, f)),
            pl.BlockSpec((d, FFN_CHUNK), lambda i, f: (0, f)),
            pl.BlockSpec((CONV_WIDTH, FFN_CHUNK), lambda i, f: (0, f)),
            pl.BlockSpec((CONV_WIDTH, FFN_CHUNK), lambda i, f: (0, f)),
            pl.BlockSpec((1, FFN_CHUNK), lambda i, f: (0, f)),
            pl.BlockSpec((1, FFN_CHUNK), lambda i, f: (0, f)),
            pl.BlockSpec((FFN_CHUNK, d), lambda i, f: (f, 0)),
            pl.BlockSpec((1, d), lambda i, f: (0, 0)),
        ],
        out_specs=pl.BlockSpec((tm, d), lambda i, f: (i, 0)),
        scratch_shapes=[
            pltpu.VMEM((HALO + tm, d), jnp.bfloat16),
            pltpu.VMEM((tm, d), jnp.float32),
        ],
        compiler_params=_cparams(("parallel", "arbitrary")),
        name="conv_glu_ffn",
    )(x2d, x2d, g.reshape(1, d), wg, wv, cwg, cwv, cbg, cbv, wo, fg)


def _rope_tables(seq):
    half = ROT_DIM // 2
    inv = ROPE_THETA ** (-jnp.arange(half, dtype=jnp.float32) / half)
    ang = jnp.arange(seq).astype(jnp.float32)[:, None] * inv[None, :]
    cos, sin = jnp.cos(ang), jnp.sin(ang)
    ones = jnp.ones((seq, HEAD_DIM - ROT_DIM), jnp.float32)
    zeros = jnp.zeros((seq, HEAD_DIM - ROT_DIM), jnp.float32)
    z8 = jnp.zeros((seq, half), jnp.float32)
    c = jnp.concatenate([cos, cos, ones], axis=1)
    s_lo = jnp.concatenate([-sin, z8, zeros], axis=1)
    s_hi = jnp.concatenate([z8, sin, zeros], axis=1)
    tab = jnp.stack([c, s_lo, s_hi])
    return jnp.concatenate([tab] * (LANES // HEAD_DIM), axis=2)


def _triangle_biases():
    rk = jnp.arange(ATT_T)[:, None]
    rq = jnp.arange(ATT_T)[None, :]
    causal = jnp.where(rk <= rq, 0.0, MASK_BIAS)
    lower = jnp.where(rk > rq, 0.0, MASK_BIAS)
    return jnp.stack([causal, lower]).astype(jnp.float32)


def _pad_cols(w, n):
    return jnp.pad(w, ((0, 0), (0, n - w.shape[1])))


def _vT_aug(v, bsz, seq, heads, dv):
    nt = seq // ATT_T
    vT = v.reshape(bsz, nt, ATT_T, heads, dv).transpose(0, 3, 1, 4, 2)
    extra = jnp.zeros((bsz, heads, nt, V_AUG, ATT_T), v.dtype).at[:, :, :, 0, :].set(1.0)
    return jnp.concatenate([vT, extra], axis=3)


def _nsa_layer(x2d, bsz, seq, g, w_in, w_out, pe, w1, b1, w2, rope_tab, tri):
    bf = jnp.bfloat16
    hk, hd = NSA_KV_HEADS, HEAD_DIM
    wq = w_in[:, :NSA_Q_W]
    wkv = [w_in[:, NSA_Q_W + k * NSA_KV_W:NSA_Q_W + (k + 1) * NSA_KV_W] for k in range(6)]
    wg = _pad_cols(w_in[:, NSA_Q_W + 6 * NSA_KV_W:], COL_TILE)
    w_all = jnp.concatenate([wq, wq] + wkv + [wg], axis=1).astype(bf)
    nq_t = NSA_Q_W // COL_TILE
    modes = ([MODE_SCALE] * nq_t + [MODE_ROPE_SCALE] * nq_t
             + [MODE_PLAIN, MODE_PLAIN, MODE_ROPE, MODE_PLAIN, MODE_ROPE, MODE_PLAIN, MODE_SIGMOID])
    proj = norm_linear(x2d, g, w_all, modes, rope_tab, seq).reshape(bsz, seq, -1)

    off = 0
    q = proj[..., off:off + NSA_Q_W]; off += NSA_Q_W
    qr = proj[..., off:off + NSA_Q_W]; off += NSA_Q_W
    kc, vc, ks, vs, kw, vw = (proj[..., off + k * NSA_KV_W:off + (k + 1) * NSA_KV_W] for k in range(6))
    off += 6 * NSA_KV_W
    gates = proj[..., off:off + 3 * NSA_HEADS]

    qT = q.reshape(bsz, seq, NSA_HEADS, hd).transpose(0, 2, 3, 1)
    qrT = qr.reshape(bsz, seq, NSA_HEADS, hd).transpose(0, 2, 3, 1)

    nch = seq // CMP_STRIDE
    def chunked(a):
        return a.reshape(bsz, nch, CMP_STRIDE, hk, hd).transpose(0, 3, 1, 2, 4).reshape(bsz, hk, nch, CMP_STRIDE * hd)
    chunks = jnp.stack([chunked(kc), chunked(vc)])
    pe_flat8 = jnp.broadcast_to(pe.reshape(2, 1, CMP_BLOCK * hd), (2, 8, CMP_BLOCK * hd)).astype(bf)
    cmp = compress(chunks, pe_flat8, w1.astype(bf), b1.reshape(2, 1, PHI_HIDDEN), w2.astype(bf))
    kcmp = cmp[0]
    vcmpT = cmp[1].transpose(0, 1, 3, 2)

    n_sel = seq // SEL_BLOCK
    ci = jnp.arange(nch)[None, :]
    sj = jnp.arange(n_sel)[:, None]
    ovT = ((ci * CMP_STRIDE < (sj + 1) * SEL_BLOCK) & (ci * CMP_STRIDE + CMP_BLOCK > sj * SEL_BLOCK)
           & (ci < nch - 1)).astype(bf)

    nt = seq // ATT_T
    ks_h = ks.reshape(bsz, seq, hk, hd).transpose(0, 2, 1, 3)
    onehot = (jnp.arange(seq)[:, None] // SEL_BLOCK == jnp.arange(n_sel)[None, :]).astype(bf)
    kaug = jnp.concatenate(
        [ks_h, jnp.zeros((bsz, hk, seq, LANES - hd), bf),
         jnp.broadcast_to(onehot, (bsz, hk, seq, n_sel))], axis=-1)
    kaug = kaug.reshape(bsz, hk, nt, ATT_T, LANES + n_sel)
    vsT = _vT_aug(vs, bsz, seq, hk, hd)
    kw_t = kw.reshape(bsz, nt, ATT_T, hk, hd).transpose(0, 3, 1, 2, 4)
    vwT = _vT_aug(vw, bsz, seq, hk, hd)
    gates_t = gates.reshape(bsz, nt, ATT_T, hk, NSA_GROUP, 3).transpose(0, 3, 1, 5, 4, 2)
    gates_t = gates_t.reshape(bsz, hk, nt, 3, NSA_GROUP * ATT_T).astype(jnp.float32)

    o = nsa_attention(qT, qrT, kcmp, vcmpT, ovT, kaug, vsT, kw_t, vwT, tri, gates_t, seq)
    o = o.reshape(bsz, hk, nt, hd, NSA_GROUP, ATT_T).transpose(0, 2, 5, 1, 4, 3)
    o = o.reshape(bsz * seq, NSA_Q_W)
    return res_linear(x2d, o, w_out.astype(bf))


def _shared_kv(x2d, bsz, seq, g, kv_w, rope_tab):
    nk = DIFF_Q_W // COL_TILE
    nv = DIFF_V_W // COL_TILE
    kv = norm_linear(x2d, g, kv_w.astype(jnp.bfloat16), [MODE_ROPE] * nk + [MODE_PLAIN] * nv, rope_tab, seq)
    kv = kv.reshape(bsz, seq, -1)
    k = kv[..., :DIFF_Q_W].reshape(bsz, seq // ATT_T, ATT_T, DIFF_HEADS, 2 * DIFF_QK_DIM)
    kcat = k.transpose(0, 3, 1, 2, 4)
    vT = _vT_aug(kv[..., DIFF_Q_W:], bsz, seq, DIFF_HEADS, DIFF_V_DIM)
    return kcat, vT


def _diff_layer(x2d, bsz, seq, g, w_q, lam_vec, subln_g, w_out, kcat, vT, lam_init, rope_tab, tri):
    bf = jnp.bfloat16
    q = norm_linear(x2d, g, w_q.astype(bf), [MODE_ROPE_SCALE] * (DIFF_Q_W // COL_TILE), rope_tab, seq)
    qT = q.reshape(bsz, seq, DIFF_HEADS, 2 * DIFF_QK_DIM).transpose(0, 2, 3, 1)
    g_rows = jnp.broadcast_to(subln_g.astype(jnp.float32)[:, None], (DIFF_V_DIM, LANES))
    o = diff_attention(qT, kcat, vT, tri, lam_vec.astype(jnp.float32), g_rows, lam_init, seq)
    o = o.transpose(0, 2, 4, 1, 3).reshape(bsz * seq, DIFF_V_W)
    return res_linear(x2d, o, w_out.astype(bf))


def _ffn_layer(x2d, seq, g, w_in, conv_w, conv_b, w_out, final_g):
    bf = jnp.bfloat16
    f = FFN_DIM
    wg = _pad_cols(w_in[:, :f], FFN_PAD).astype(bf)
    wv = _pad_cols(w_in[:, f:], FFN_PAD).astype(bf)
    cwg = _pad_cols(conv_w[:, :f], FFN_PAD)
    cwv = _pad_cols(conv_w[:, f:], FFN_PAD)
    cbg = _pad_cols(conv_b[None, :f], FFN_PAD)
    cbv = _pad_cols(conv_b[None, f:], FFN_PAD)
    wo = jnp.pad(w_out, ((0, FFN_PAD - f), (0, 0))).astype(bf)
    return ffn(x2d, g, wg, wv, cwg, cwv, cbg, cbv, wo, final_g, seq)


def kernel(x, attn_norm_g, ffn_norm_g, nsa_w_in, nsa_w_out, nsa_phi_pe, nsa_phi_w1, nsa_phi_b1, nsa_phi_w2,
           kv_norm_g, kv_w, diff_w_q, diff_lambda, diff_subln_g, diff_w_out, ffn_w_in, ffn_conv_w, ffn_conv_b,
           ffn_w_out, final_norm_g):
    bsz, seq, d = x.shape
    x2d = x.reshape(bsz * seq, d)
    rope_tab = _rope_tables(seq)
    tri = _triangle_biases()
    kcat = vT = None
    for l in range(DEPTH):
        if l < N_A_LAYERS:
            x2d = _nsa_layer(x2d, bsz, seq, attn_norm_g[l], nsa_w_in[l], nsa_w_out[l], nsa_phi_pe[l],
                             nsa_phi_w1[l], nsa_phi_b1[l], nsa_phi_w2[l], rope_tab, tri)
        else:
            j = l - N_A_LAYERS
            lam_init = 0.8 - 0.6 * math.exp(-0.3 * l)
            x2d = _diff_layer(x2d, bsz, seq, attn_norm_g[l], diff_w_q[j], diff_lambda[j], diff_subln_g[j],
                              diff_w_out[j], kcat, vT, lam_init, rope_tab, tri)
        final_g = final_norm_g if l == DEPTH - 1 else None
        x2d = _ffn_layer(x2d, seq, ffn_norm_g[l], ffn_w_in[l], ffn_conv_w[l], ffn_conv_b[l], ffn_w_out[l], final_g)
        if l == N_A_LAYERS - 1:
            kcat, vT = _shared_kv(x2d, bsz, seq, kv_norm_g, kv_w, rope_tab)
    return x2d.reshape(bsz, seq, d)
```

```python
import functools
import math

import jax
import jax.numpy as jnp
from jax import lax
from jax.experimental import pallas as pl
from jax.experimental.pallas import tpu as pltpu

D_MODEL = 1024
DEPTH = 4
N_A_LAYERS = DEPTH // 2
NSA_HEADS = 16
NSA_KV_HEADS = 4
NSA_GROUP = NSA_HEADS // NSA_KV_HEADS
HEAD_DIM = 64
CMP_BLOCK = 32
CMP_STRIDE = 16
SEL_BLOCK = 64
SEL_TOP_N = 16
N_FORCED = 3
WINDOW = 512
PHI_HIDDEN = 256
NSA_Q_W = NSA_HEADS * HEAD_DIM
NSA_KV_W = NSA_KV_HEADS * HEAD_DIM
DIFF_HEADS = 8
DIFF_QK_DIM = 64
DIFF_V_DIM = 128
DIFF_Q_W = DIFF_HEADS * 2 * DIFF_QK_DIM
DIFF_V_W = DIFF_HEADS * DIFF_V_DIM
ROT_DIM = HEAD_DIM // 4
ROPE_THETA = 500000.0
FFN_DIM = 2752
CONV_WIDTH = 3
EPS = 1e-6
NEG = -1e30
FORCE = 1e4

LANES = 128
SUBLANES = 8
MXU_DIM = 256
VMEM_LIMIT_BYTES = 56 * 1024 * 1024

COL_TILE = MXU_DIM
ROW_TILE = 512
ATT_T = 256
DIFF_HPS = 4
NSA_HPS = 2
V_AUG = SUBLANES
FFN_PAD = 2816
FFN_CHUNK = FFN_PAD // 2
MASK_BIAS = -1e9
NEG_FLOOR = -1e20
REMOVED = -3.0e38

MODE_PLAIN, MODE_SCALE, MODE_ROPE, MODE_ROPE_SCALE, MODE_SIGMOID = range(5)
LOG2E = math.log2(math.e)
QK_SCALE = HEAD_DIM ** -0.5 * LOG2E


def _cparams(sem):
    return pltpu.CompilerParams(dimension_semantics=sem, vmem_limit_bytes=VMEM_LIMIT_BYTES)


def _rms(x, g):
    y = x * lax.rsqrt(jnp.mean(x * x, axis=-1, keepdims=True) + EPS)
    return y * g


def _dot(a, b):
    return jnp.dot(a, b, preferred_element_type=jnp.float32)


def _norm_linear_kernel(x_ref, g_ref, w_ref, rope_ref, o_ref, *, modes):
    h = _rms(x_ref[...], g_ref[...]).astype(jnp.bfloat16)
    for j, mode in enumerate(modes):
        c0 = j * COL_TILE
        y = _dot(h, w_ref[:, c0:c0 + COL_TILE])
        if mode in (MODE_ROPE, MODE_ROPE_SCALE):
            cos, s_lo, s_hi = rope_ref[0], rope_ref[1], rope_ref[2]
            parts = []
            for c in range(COL_TILE // LANES):
                yh = y[:, c * LANES:(c + 1) * LANES]
                up = pltpu.roll(yh, LANES - ROT_DIM // 2, 1)
                dn = pltpu.roll(yh, ROT_DIM // 2, 1)
                parts.append(yh * cos + up * s_lo + dn * s_hi)
            y = jnp.concatenate(parts, axis=1)
        if mode in (MODE_SCALE, MODE_ROPE_SCALE):
            y = y * QK_SCALE
        if mode == MODE_SIGMOID:
            y = jax.nn.sigmoid(y)
        o_ref[:, c0:c0 + COL_TILE] = y.astype(o_ref.dtype)


def norm_linear(x2d, g, w, modes, rope_tab, seq):
    m, d = x2d.shape
    n = w.shape[1]
    assert n == len(modes) * COL_TILE and m % ROW_TILE == 0 and seq % ROW_TILE == 0
    tiles_per_seq = seq // ROW_TILE
    return pl.pallas_call(
        functools.partial(_norm_linear_kernel, modes=tuple(modes)),
        out_shape=jax.ShapeDtypeStruct((m, n), jnp.bfloat16),
        grid=(m // ROW_TILE,),
        in_specs=[
            pl.BlockSpec((ROW_TILE, d), lambda i: (i, 0)),
            pl.BlockSpec((1, d), lambda i: (0, 0)),
            pl.BlockSpec((d, n), lambda i: (0, 0)),
            pl.BlockSpec((3, ROW_TILE, LANES), lambda i: (0, i % tiles_per_seq, 0)),
        ],
        out_specs=pl.BlockSpec((ROW_TILE, n), lambda i: (i, 0)),
        compiler_params=_cparams(("parallel",)),
        name="norm_linear",
    )(x2d, g.reshape(1, d), w, rope_tab)


def _res_linear_kernel(x_ref, a_ref, w_ref, o_ref):
    o_ref[...] = x_ref[...] + _dot(a_ref[...], w_ref[...])


def res_linear(x2d, a2d, w):
    m, d = x2d.shape
    k = a2d.shape[1]
    return pl.pallas_call(
        _res_linear_kernel,
        out_shape=jax.ShapeDtypeStruct((m, d), jnp.float32),
        grid=(m // ROW_TILE,),
        in_specs=[
            pl.BlockSpec((ROW_TILE, d), lambda i: (i, 0)),
            pl.BlockSpec((ROW_TILE, k), lambda i: (i, 0)),
            pl.BlockSpec((k, d), lambda i: (0, 0)),
        ],
        out_specs=pl.BlockSpec((ROW_TILE, d), lambda i: (i, 0)),
        compiler_params=_cparams(("parallel",)),
        name="res_linear",
    )(x2d, a2d, w)


def _compress_kernel(ch_ref, pe_ref, w1_ref, b1_ref, w2_ref, o_ref):
    ch = ch_ref[0, 0, 0]
    w1 = w1_ref[0]
    half = CMP_STRIDE * HEAD_DIM
    nc = ch.shape[0]
    first = _dot(ch, w1[:half])
    second = _dot(ch, w1[half:])
    second = pltpu.roll(second, nc - 1, 0)
    bias = _dot(pe_ref[0], w1)[0:1] + b1_ref[0]
    pre = first + second + bias
    hid = jax.nn.gelu(pre, approximate=True)
    o_ref[0, 0, 0] = _dot(hid.astype(jnp.bfloat16), w2_ref[0]).astype(o_ref.dtype)


def compress(chunks, pe_flat8, w1, b1, w2):
    _, b, hk, nc, cw = chunks.shape
    return pl.pallas_call(
        _compress_kernel,
        out_shape=jax.ShapeDtypeStruct((2, b, hk, nc, HEAD_DIM), jnp.bfloat16),
        grid=(2, b, hk),
        in_specs=[
            pl.BlockSpec((1, 1, 1, nc, cw), lambda s, i, j: (s, i, j, 0, 0)),
            pl.BlockSpec((1, 8, 2 * cw), lambda s, i, j: (s, 0, 0)),
            pl.BlockSpec((1, 2 * cw, PHI_HIDDEN), lambda s, i, j: (s, 0, 0)),
            pl.BlockSpec((1, 1, PHI_HIDDEN), lambda s, i, j: (s, 0, 0)),
            pl.BlockSpec((1, PHI_HIDDEN, HEAD_DIM), lambda s, i, j: (s, 0, 0)),
        ],
        out_specs=pl.BlockSpec((1, 1, 1, nc, HEAD_DIM), lambda s, i, j: (s, i, j, 0, 0)),
        compiler_params=_cparams(("parallel", "parallel", "parallel")),
        name="nsa_compress",
    )(chunks, pe_flat8, w1, b1, w2)


def _softmax_pv(s, vT_tile, m_ref, acc_ref):
    m_old = m_ref[...]
    m_new = jnp.maximum(m_old, jnp.max(s, axis=0, keepdims=True))
    alpha = jnp.exp2(m_old - m_new)
    p = jnp.exp2(s - m_new).astype(jnp.bfloat16)
    acc_ref[...] = alpha * acc_ref[...] + _dot(vT_tile, p)
    m_ref[...] = m_new


def _nsa_attn_kernel(qT_ref, qrT_ref, kc_ref, vcT_ref, ovT_ref, kaug_ref, vsT_ref, kw_ref, vwT_ref, tri_ref,
                     gate_ref, o_ref, m_s, acc_s, s_buf, *, n_sel, top_n):
    g_heads = NSA_GROUP
    tq = ATT_T
    cols = g_heads * tq
    i = pl.program_id(2)
    qs = i * tq
    heads = range(NSA_HPS)
    def heads_T(ref):
        allT = ref[...].astype(jnp.float32).T
        return [jnp.concatenate([allT[(hh * g_heads + g) * HEAD_DIM:(hh * g_heads + g + 1) * HEAD_DIM]
                                 for g in range(g_heads)], axis=1).astype(jnp.bfloat16) for hh in heads]

    qT = heads_T(qT_ref)
    qrT = heads_T(qrT_ref)
    t_col = qs + (lax.broadcasted_iota(jnp.int32, (1, cols), 1) & (tq - 1))
    causal_bias = jnp.concatenate([tri_ref[0]] * g_heads, axis=1)
    lower_bias = jnp.concatenate([tri_ref[1]] * g_heads, axis=1)

    nc = kc_ref.shape[2]
    i0 = jnp.maximum(i - 2, 0)
    i1 = jnp.maximum(i - 1, 0)
    sc = [_dot(kc_ref[0, hh], qT[hh]) for hh in heads]
    sw = [(_dot(kw_ref[0, hh, i0], qrT[hh]) + lower_bias,
           _dot(kw_ref[0, hh, i1], qrT[hh]),
           _dot(kw_ref[0, hh, i], qrT[hh]) + causal_bias) for hh in heads]

    c_end = lax.broadcasted_iota(jnp.int32, (nc, 1), 0) * CMP_STRIDE + (CMP_BLOCK - 1)
    ovT = ovT_ref[...]
    o_c, imp = [], []
    for hh in heads:
        s = jnp.where(c_end <= t_col, sc[hh], NEG)
        mc = jnp.maximum(jnp.max(s, axis=0, keepdims=True), NEG_FLOOR)
        ec = jnp.exp2(s - mc)
        lc = jnp.sum(ec, axis=0, keepdims=True)
        pc = ec * jnp.where(lc > 0.0, 1.0 / lc, 0.0)
        o_c.append(_dot(vcT_ref[0, hh], pc.astype(jnp.bfloat16)))
        psum = pc[:, 0:tq]
        for g in range(1, g_heads):
            psum = psum + pc[:, g * tq:(g + 1) * tq]
        p_hi = psum.astype(jnp.bfloat16)
        p_lo = (psum - p_hi.astype(jnp.float32)).astype(jnp.bfloat16)
        imp.append(_dot(ovT, p_hi) + _dot(ovT, p_lo))

    o_w = []
    for hh in heads:
        s0, s1, s2 = sw[hh]
        mw = jnp.maximum(jnp.maximum(s0.max(axis=0, keepdims=True), s1.max(axis=0, keepdims=True)),
                         s2.max(axis=0, keepdims=True))
        r0 = _dot(vwT_ref[0, hh, i0], jnp.exp2(s0 - mw).astype(jnp.bfloat16)) * jnp.where(i >= 2, 1.0, 0.0)
        r1 = _dot(vwT_ref[0, hh, i1], jnp.exp2(s1 - mw).astype(jnp.bfloat16)) * jnp.where(i >= 1, 1.0, 0.0)
        r2 = _dot(vwT_ref[0, hh, i], jnp.exp2(s2 - mw).astype(jnp.bfloat16))
        accw = r0 + r1 + r2
        o_w.append(accw[:HEAD_DIM] * (1.0 / accw[HEAD_DIM:HEAD_DIM + 1]))

    blk = lax.broadcasted_iota(jnp.int32, (n_sel, tq), 0)
    blk_f = blk.astype(jnp.float32)
    cur = (qs + lax.broadcasted_iota(jnp.int32, (1, tq), 1)) >> int(math.log2(SEL_BLOCK))
    work = [jnp.where(blk == 0, NEG, jnp.where(blk >= cur - 1, NEG, imp[hh])) for hh in heads]
    for _ in range(top_n - N_FORCED):
        for hh in heads:
            top = jnp.max(work[hh], axis=0, keepdims=True)
            first = jnp.min(jnp.where(work[hh] == top, blk_f, float(n_sel)), axis=0, keepdims=True)
            work[hh] = jnp.where(blk_f == first, REMOVED, work[hh])
    q_aug = []
    for hh in heads:
        picked = jnp.where(work[hh] == REMOVED, 0.0, MASK_BIAS)
        picked = jnp.where(blk == 0, 0.0, jnp.where(blk >= cur - 1, 0.0, picked))
        selbias = jnp.where(blk <= cur, picked, MASK_BIAS).astype(jnp.bfloat16)
        q_aug.append(jnp.concatenate(
            [qrT[hh], jnp.zeros((LANES - HEAD_DIM, cols), jnp.bfloat16),
             jnp.concatenate([selbias] * g_heads, axis=1)], axis=0))

    m_s[...] = jnp.full(m_s.shape, NEG, jnp.float32)
    acc_s[...] = jnp.zeros(acc_s.shape, jnp.float32)

    for hh in heads:
        s_buf[hh] = _dot(kaug_ref[0, hh, 0], q_aug[hh])

    def body(kt, carry):
        s_cur = [s_buf[hh] for hh in heads]
        for hh in heads:
            s_buf[hh] = _dot(kaug_ref[0, hh, kt + 1], q_aug[hh])
        for hh in heads:
            _softmax_pv(s_cur[hh], vsT_ref[0, hh, kt], m_s.at[hh], acc_s.at[hh])
        return carry

    lax.fori_loop(0, i, body, 0)
    for hh in heads:
        _softmax_pv(s_buf[hh] + causal_bias, vsT_ref[0, hh, i], m_s.at[hh], acc_s.at[hh])
    rows = []
    for hh in heads:
        acc = acc_s[hh]
        o_s = acc[:HEAD_DIM] * (1.0 / acc[HEAD_DIM:HEAD_DIM + 1])
        gate = gate_ref[0, hh, 0]
        out = gate[0:1] * o_c[hh] + gate[1:2] * o_s + gate[2:3] * o_w[hh]
        rows += [out[:, g * tq:(g + 1) * tq] for g in range(g_heads)]
    o_ref[...] = jnp.concatenate(rows, axis=0).T.astype(o_ref.dtype)


def nsa_attention(proj2d, kcmp, vcmpT, ovT, kaug, vsT, kw, vwT, tri, gates, seq):
    b = kcmp.shape[0]
    hk = NSA_KV_HEADS
    qw = NSA_HPS * NSA_GROUP * HEAD_DIM
    nq = seq // ATT_T
    n_sel = seq // SEL_BLOCK
    top_n = min(SEL_TOP_N, n_sel)
    assert top_n > N_FORCED
    nc = kcmp.shape[2]
    cols = NSA_GROUP * ATT_T
    kd = kaug.shape[-1]
    dv = HEAD_DIM + V_AUG
    hps = NSA_HPS
    resident = pl.Buffered(1)
    kern = functools.partial(_nsa_attn_kernel, n_sel=n_sel, top_n=top_n)
    return pl.pallas_call(
        kern,
        out_shape=jax.ShapeDtypeStruct((b * seq, NSA_Q_W), jnp.bfloat16),
        grid=(b, hk // hps, nq),
        in_specs=[
            pl.BlockSpec((ATT_T, qw), lambda bi, h, i: (bi * nq + i, h)),
            pl.BlockSpec((ATT_T, qw), lambda bi, h, i: (bi * nq + i, NSA_Q_W // qw + h)),
            pl.BlockSpec((1, hps, nc, HEAD_DIM), lambda bi, h, i: (bi, h, 0, 0)),
            pl.BlockSpec((1, hps, HEAD_DIM, nc), lambda bi, h, i: (bi, h, 0, 0)),
            pl.BlockSpec((n_sel, nc), lambda bi, h, i: (0, 0)),
            pl.BlockSpec((1, hps, nq, ATT_T, kd), lambda bi, h, i: (bi, h, 0, 0, 0), pipeline_mode=resident),
            pl.BlockSpec((1, hps, nq, dv, ATT_T), lambda bi, h, i: (bi, h, 0, 0, 0), pipeline_mode=resident),
            pl.BlockSpec((1, hps, nq, ATT_T, HEAD_DIM), lambda bi, h, i: (bi, h, 0, 0, 0), pipeline_mode=resident),
            pl.BlockSpec((1, hps, nq, dv, ATT_T), lambda bi, h, i: (bi, h, 0, 0, 0), pipeline_mode=resident),
            pl.BlockSpec((2, ATT_T, ATT_T), lambda bi, h, i: (0, 0, 0)),
            pl.BlockSpec((1, hps, 1, 3, cols), lambda bi, h, i: (bi, h, i, 0, 0)),
        ],
        out_specs=pl.BlockSpec((ATT_T, qw), lambda bi, h, i: (bi * nq + i, h)),
        scratch_shapes=[
            pltpu.VMEM((hps, 1, cols), jnp.float32),
            pltpu.VMEM((hps, dv, cols), jnp.float32),
            pltpu.VMEM((hps, ATT_T, cols), jnp.float32),
        ],
        compiler_params=_cparams(("parallel", "parallel", "arbitrary")),
        name="nsa_attention",
    )(proj2d, proj2d, kcmp, vcmpT, ovT, kaug, vsT, kw, vwT, tri, gates)


def _diff_attn_kernel(qT_ref, k_ref, vT_ref, tri_ref, lam_ref, g_ref, o_ref, m_s, acc_s, s_buf, *, lam_init):
    tq = ATT_T
    i = pl.program_id(2)
    zero = jnp.zeros((DIFF_QK_DIM, tq), jnp.bfloat16)
    q_bd = []
    q_allT = qT_ref[...].astype(jnp.float32).T.astype(jnp.bfloat16)
    hw = 2 * DIFF_QK_DIM
    for hh in range(DIFF_HPS):
        q2 = q_allT[hh * hw:(hh + 1) * hw]
        q_bd.append(jnp.concatenate(
            [jnp.concatenate([q2[:DIFF_QK_DIM], zero], axis=1),
             jnp.concatenate([zero, q2[DIFF_QK_DIM:]], axis=1)], axis=0))
    causal_bias = jnp.concatenate([tri_ref[0]] * 2, axis=1)

    m_s[...] = jnp.full(m_s.shape, NEG, jnp.float32)
    acc_s[...] = jnp.zeros(acc_s.shape, jnp.float32)

    for hh in range(DIFF_HPS):
        s_buf[hh] = _dot(k_ref[0, hh, 0], q_bd[hh])

    def body(kt, carry):
        s_cur = [s_buf[hh] for hh in range(DIFF_HPS)]
        for hh in range(DIFF_HPS):
            s_buf[hh] = _dot(k_ref[0, hh, kt + 1], q_bd[hh])
        for hh in range(DIFF_HPS):
            _softmax_pv(s_cur[hh], vT_ref[0, hh, kt], m_s.at[hh], acc_s.at[hh])
        return carry

    lax.fori_loop(0, i, body, 0)
    for hh in range(DIFF_HPS):
        _softmax_pv(s_buf[hh] + causal_bias, vT_ref[0, hh, i], m_s.at[hh], acc_s.at[hh])

    lv = lam_ref[...]
    lam = (jnp.exp(jnp.sum(lv[0:1] * lv[1:2], axis=1, keepdims=True))
           - jnp.exp(jnp.sum(lv[2:3] * lv[3:4], axis=1, keepdims=True)) + lam_init)
    g = jnp.concatenate([g_ref[...]] * (tq // LANES), axis=1)
    rows = []
    for hh in range(DIFF_HPS):
        acc = acc_s[hh]
        o = acc[:DIFF_V_DIM] * (1.0 / acc[DIFF_V_DIM:DIFF_V_DIM + 1])
        a = o[:, :tq] - lam * o[:, tq:]
        y = a * lax.rsqrt(jnp.mean(a * a, axis=0, keepdims=True) + EPS)
        rows.append((y * g) * (1.0 - lam_init))
    o_ref[...] = jnp.concatenate(rows, axis=0).T.astype(o_ref.dtype)


def diff_attention(q2d, kcat, vT, tri, lam_vec, g_rows, lam_init, seq):
    b = kcat.shape[0]
    nq = seq // ATT_T
    cols = 2 * ATT_T
    dv = DIFF_V_DIM + V_AUG
    hps = DIFF_HPS
    return pl.pallas_call(
        functools.partial(_diff_attn_kernel, lam_init=lam_init),
        out_shape=jax.ShapeDtypeStruct((b * seq, DIFF_V_W), jnp.bfloat16),
        grid=(b, DIFF_HEADS // hps, nq),
        in_specs=[
            pl.BlockSpec((ATT_T, hps * 2 * DIFF_QK_DIM), lambda bi, h, i: (bi * nq + i, h)),
            pl.BlockSpec((1, hps, nq, ATT_T, 2 * DIFF_QK_DIM), lambda bi, h, i: (bi, h, 0, 0, 0),
                         pipeline_mode=pl.Buffered(1)),
            pl.BlockSpec((1, hps, nq, dv, ATT_T), lambda bi, h, i: (bi, h, 0, 0, 0),
                         pipeline_mode=pl.Buffered(1)),
            pl.BlockSpec((2, ATT_T, ATT_T), lambda bi, h, i: (0, 0, 0)),
            pl.BlockSpec((4, DIFF_QK_DIM), lambda bi, h, i: (0, 0)),
            pl.BlockSpec((DIFF_V_DIM, LANES), lambda bi, h, i: (0, 0)),
        ],
        out_specs=pl.BlockSpec((ATT_T, hps * DIFF_V_DIM), lambda bi, h, i: (bi * nq + i, h)),
        scratch_shapes=[
            pltpu.VMEM((hps, 1, cols), jnp.float32),
            pltpu.VMEM((hps, dv, cols), jnp.float32),
            pltpu.VMEM((hps, ATT_T, cols), jnp.float32),
        ],
        compiler_params=_cparams(("parallel", "parallel", "arbitrary")),
        name="diff_attention",
    )(q2d, kcat, vT, tri, lam_vec, g_rows)


HALO = SUBLANES


def _ffn_kernel(x_ref, xh_ref, g_ref, wg_ref, wv_ref, cwg_ref, cwv_ref, cbg_ref, cbv_ref, wo_ref, fg_ref,
                o_ref, h_s, acc_s, *, tiles_per_seq, final_norm):
    i = pl.program_id(0)
    f = pl.program_id(1)

    @pl.when(f == 0)
    def _():
        g = g_ref[...]
        h_s[HALO:, :] = _rms(x_ref[...], g).astype(jnp.bfloat16)
        hh = _rms(xh_ref[...], g)
        hh = jnp.where(i % tiles_per_seq == 0, 0.0, hh)
        h_s[:HALO, :] = hh.astype(jnp.bfloat16)
        acc_s[...] = jnp.zeros(acc_s.shape, jnp.float32)

    h = h_s[...]

    def conv(u, cw_ref, cb_ref):
        cw = cw_ref[...]
        c = cb_ref[...] + u[HALO:] * cw[2:3]
        c = c + pltpu.roll(u, 1, 0)[HALO:] * cw[1:2]
        c = c + pltpu.roll(u, 2, 0)[HALO:] * cw[0:1]
        return c

    gate = conv(_dot(h, wg_ref[...]), cwg_ref, cbg_ref)
    val = conv(_dot(h, wv_ref[...]), cwv_ref, cbv_ref)
    act = (gate * jax.nn.sigmoid(gate)) * val
    acc_s[...] += _dot(act.astype(jnp.bfloat16), wo_ref[...])

    @pl.when(f == pl.num_programs(1) - 1)
    def _():
        y = x_ref[...] + acc_s[...]
        if final_norm:
            y = _rms(y, fg_ref[...])
        o_ref[...] = y


def ffn(x2d, g, wg, wv, cwg, cwv, cbg, cbv, wo, final_g, seq):
    m, d = x2d.shape
    tm = ROW_TILE
    nf = FFN_PAD // FFN_CHUNK
    tiles_per_seq = seq // tm
    final_norm = final_g is not None
    fg = (final_g if final_norm else g).reshape(1, d)
    kern = functools.partial(_ffn_kernel, tiles_per_seq=tiles_per_seq, final_norm=final_norm)
    return pl.pallas_call(
        kern,
        out_shape=jax.ShapeDtypeStruct((m, d), jnp.float32),
        grid=(m // tm, nf),
        in_specs=[
            pl.BlockSpec((tm, d), lambda i, f: (i, 0)),
            pl.BlockSpec((HALO, d), lambda i, f: (jnp.maximum(i * (tm // HALO) - 1, 0), 0)),
            pl.BlockSpec((1, d), lambda i, f: (0, 0)),
            pl.BlockSpec((d, FFN_CHUNK), lambda i, f: (0, f)),
            pl.BlockSpec((d, FFN_CHUNK), lambda i, f: (0, f)),
            pl.BlockSpec((CONV_WIDTH, FFN_CHUNK), lambda i, f: (0, f)),
            pl.BlockSpec((CONV_WIDTH, FFN_CHUNK), lambda i, f: (0, f)),
            pl.BlockSpec((1, FFN_CHUNK), lambda i, f: (0, f)),
            pl.BlockSpec((1, FFN_CHUNK), lambda i, f: (0, f)),
            pl.BlockSpec((FFN_CHUNK, d), lambda i, f: (f, 0)),
            pl.BlockSpec((1, d), lambda i, f: (0, 0)),
        ],
        out_specs=pl.BlockSpec((tm, d), lambda i, f: (i, 0)),
        scratch_shapes=[
            pltpu.VMEM((HALO + tm, d), jnp.bfloat16),
            pltpu.VMEM((tm, d), jnp.float32),
        ],
        compiler_params=_cparams(("parallel", "arbitrary")),
        name="conv_glu_ffn",
    )(x2d, x2d, g.reshape(1, d), wg, wv, cwg, cwv, cbg, cbv, wo, fg)


def _rope_tables(seq):
    half = ROT_DIM // 2
    inv = ROPE_THETA ** (-jnp.arange(half, dtype=jnp.float32) / half)
    ang = jnp.arange(seq).astype(jnp.float32)[:, None] * inv[None, :]
    cos, sin = jnp.cos(ang), jnp.sin(ang)
    ones = jnp.ones((seq, HEAD_DIM - ROT_DIM), jnp.float32)
    zeros = jnp.zeros((seq, HEAD_DIM - ROT_DIM), jnp.float32)
    z8 = jnp.zeros((seq, half), jnp.float32)
    c = jnp.concatenate([cos, cos, ones], axis=1)
    s_lo = jnp.concatenate([-sin, z8, zeros], axis=1)
    s_hi = jnp.concatenate([z8, sin, zeros], axis=1)
    tab = jnp.stack([c, s_lo, s_hi])
    return jnp.concatenate([tab] * (LANES // HEAD_DIM), axis=2)


def _triangle_biases():
    rk = jnp.arange(ATT_T)[:, None]
    rq = jnp.arange(ATT_T)[None, :]
    causal = jnp.where(rk <= rq, 0.0, MASK_BIAS)
    lower = jnp.where(rk > rq, 0.0, MASK_BIAS)
    return jnp.stack([causal, lower]).astype(jnp.float32)


def _pad_cols(w, n):
    return jnp.pad(w, ((0, 0), (0, n - w.shape[1])))


def _vT_aug(v, bsz, seq, heads, dv):
    nt = seq // ATT_T
    vT = v.reshape(bsz, nt, ATT_T, heads, dv).transpose(0, 3, 1, 4, 2)
    extra = jnp.zeros((bsz, heads, nt, V_AUG, ATT_T), v.dtype).at[:, :, :, 0, :].set(1.0)
    return jnp.concatenate([vT, extra], axis=3)


def _nsa_layer(x2d, bsz, seq, g, w_in, w_out, pe, w1, b1, w2, rope_tab, tri):
    bf = jnp.bfloat16
    hk, hd = NSA_KV_HEADS, HEAD_DIM
    wq = w_in[:, :NSA_Q_W]
    wkv = [w_in[:, NSA_Q_W + k * NSA_KV_W:NSA_Q_W + (k + 1) * NSA_KV_W] for k in range(6)]
    wg = _pad_cols(w_in[:, NSA_Q_W + 6 * NSA_KV_W:], COL_TILE)
    w_all = jnp.concatenate([wq, wq] + wkv + [wg], axis=1).astype(bf)
    nq_t = NSA_Q_W // COL_TILE
    modes = ([MODE_SCALE] * nq_t + [MODE_ROPE_SCALE] * nq_t
             + [MODE_PLAIN, MODE_PLAIN, MODE_ROPE, MODE_PLAIN, MODE_ROPE, MODE_PLAIN, MODE_SIGMOID])
    proj2d = norm_linear(x2d, g, w_all, modes, rope_tab, seq)
    proj = proj2d.reshape(bsz, seq, -1)

    off = 2 * NSA_Q_W
    kc, vc, ks, vs, kw, vw = (proj[..., off + k * NSA_KV_W:off + (k + 1) * NSA_KV_W] for k in range(6))
    off += 6 * NSA_KV_W
    gates = proj[..., off:off + 3 * NSA_HEADS]

    nch = seq // CMP_STRIDE
    def chunked(a):
        return a.reshape(bsz, nch, CMP_STRIDE, hk, hd).transpose(0, 3, 1, 2, 4).reshape(bsz, hk, nch, CMP_STRIDE * hd)
    chunks = jnp.stack([chunked(kc), chunked(vc)])
    pe_flat8 = jnp.broadcast_to(pe.reshape(2, 1, CMP_BLOCK * hd), (2, 8, CMP_BLOCK * hd)).astype(bf)
    cmp = compress(chunks, pe_flat8, w1.astype(bf), b1.reshape(2, 1, PHI_HIDDEN), w2.astype(bf))
    kcmp = cmp[0]
    vcmpT = cmp[1].transpose(0, 1, 3, 2)

    n_sel = seq // SEL_BLOCK
    ci = jnp.arange(nch)[None, :]
    sj = jnp.arange(n_sel)[:, None]
    ovT = ((ci * CMP_STRIDE < (sj + 1) * SEL_BLOCK) & (ci * CMP_STRIDE + CMP_BLOCK > sj * SEL_BLOCK)
           & (ci < nch - 1)).astype(bf)

    nt = seq // ATT_T
    ks_h = ks.reshape(bsz, seq, hk, hd).transpose(0, 2, 1, 3)
    onehot = (jnp.arange(seq)[:, None] // SEL_BLOCK == jnp.arange(n_sel)[None, :]).astype(bf)
    kaug = jnp.concatenate(
        [ks_h, jnp.zeros((bsz, hk, seq, LANES - hd), bf),
         jnp.broadcast_to(onehot, (bsz, hk, seq, n_sel))], axis=-1)
    kaug = kaug.reshape(bsz, hk, nt, ATT_T, LANES + n_sel)
    vsT = _vT_aug(vs, bsz, seq, hk, hd)
    kw_t = kw.reshape(bsz, nt, ATT_T, hk, hd).transpose(0, 3, 1, 2, 4)
    vwT = _vT_aug(vw, bsz, seq, hk, hd)
    gates_t = gates.reshape(bsz, nt, ATT_T, hk, NSA_GROUP, 3).transpose(0, 3, 1, 5, 4, 2)
    gates_t = gates_t.reshape(bsz, hk, nt, 3, NSA_GROUP * ATT_T).astype(jnp.float32)

    o = nsa_attention(proj2d, kcmp, vcmpT, ovT, kaug, vsT, kw_t, vwT, tri, gates_t, seq)
    return res_linear(x2d, o, w_out.astype(bf))


def _shared_kv(x2d, bsz, seq, g, kv_w, rope_tab):
    nk = DIFF_Q_W // COL_TILE
    nv = DIFF_V_W // COL_TILE
    kv = norm_linear(x2d, g, kv_w.astype(jnp.bfloat16), [MODE_ROPE] * nk + [MODE_PLAIN] * nv, rope_tab, seq)
    kv = kv.reshape(bsz, seq, -1)
    k = kv[..., :DIFF_Q_W].reshape(bsz, seq // ATT_T, ATT_T, DIFF_HEADS, 2 * DIFF_QK_DIM)
    kcat = k.transpose(0, 3, 1, 2, 4)
    vT = _vT_aug(kv[..., DIFF_Q_W:], bsz, seq, DIFF_HEADS, DIFF_V_DIM)
    return kcat, vT


def _diff_layer(x2d, bsz, seq, g, w_q, lam_vec, subln_g, w_out, kcat, vT, lam_init, rope_tab, tri):
    bf = jnp.bfloat16
    q = norm_linear(x2d, g, w_q.astype(bf), [MODE_ROPE_SCALE] * (DIFF_Q_W // COL_TILE), rope_tab, seq)
    g_rows = jnp.broadcast_to(subln_g.astype(jnp.float32)[:, None], (DIFF_V_DIM, LANES))
    o = diff_attention(q, kcat, vT, tri, lam_vec.astype(jnp.float32), g_rows, lam_init, seq)
    return res_linear(x2d, o, w_out.astype(bf))


def _ffn_layer(x2d, seq, g, w_in, conv_w, conv_b, w_out, final_g):
    bf = jnp.bfloat16
    f = FFN_DIM
    wg = _pad_cols(w_in[:, :f], FFN_PAD).astype(bf)
    wv = _pad_cols(w_in[:, f:], FFN_PAD).astype(bf)
    cwg = _pad_cols(conv_w[:, :f], FFN_PAD)
    cwv = _pad_cols(conv_w[:, f:], FFN_PAD)
    cbg = _pad_cols(conv_b[None, :f], FFN_PAD)
    cbv = _pad_cols(conv_b[None, f:], FFN_PAD)
    wo = jnp.pad(w_out, ((0, FFN_PAD - f), (0, 0))).astype(bf)
    return ffn(x2d, g, wg, wv, cwg, cwv, cbg, cbv, wo, final_g, seq)


def kernel(x, attn_norm_g, ffn_norm_g, nsa_w_in, nsa_w_out, nsa_phi_pe, nsa_phi_w1, nsa_phi_b1, nsa_phi_w2,
           kv_norm_g, kv_w, diff_w_q, diff_lambda, diff_subln_g, diff_w_out, ffn_w_in, ffn_conv_w, ffn_conv_b,
           ffn_w_out, final_norm_g):
    bsz, seq, d = x.shape
    x2d = x.reshape(bsz * seq, d)
    rope_tab = _rope_tables(seq)
    tri = _triangle_biases()
    kcat = vT = None
    for l in range(DEPTH):
        if l < N_A_LAYERS:
            x2d = _nsa_layer(x2d, bsz, seq, attn_norm_g[l], nsa_w_in[l], nsa_w_out[l], nsa_phi_pe[l],
                             nsa_phi_w1[l], nsa_phi_b1[l], nsa_phi_w2[l], rope_tab, tri)
        else:
            j = l - N_A_LAYERS
            lam_init = 0.8 - 0.6 * math.exp(-0.3 * l)
            x2d = _diff_layer(x2d, bsz, seq, attn_norm_g[l], diff_w_q[j], diff_lambda[j], diff_subln_g[j],
                              diff_w_out[j], kcat, vT, lam_init, rope_tab, tri)
        final_g = final_norm_g if l == DEPTH - 1 else None
        x2d = _ffn_layer(x2d, seq, ffn_norm_g[l], ffn_w_in[l], ffn_conv_w[l], ffn_conv_b[l], ffn_w_out[l], final_g)
        if l == N_A_LAYERS - 1:
            kcat, vT = _shared_kv(x2d, bsz, seq, kv_norm_g, kv_w, rope_tab)
    return x2d.reshape(bsz, seq, d)
```

```python
import functools
import math

import jax
import jax.numpy as jnp
from jax import lax
from jax.experimental import pallas as pl
from jax.experimental.pallas import tpu as pltpu

D_MODEL = 1024
DEPTH = 4
N_A_LAYERS = DEPTH // 2
NSA_HEADS = 16
NSA_KV_HEADS = 4
NSA_GROUP = NSA_HEADS // NSA_KV_HEADS
HEAD_DIM = 64
CMP_BLOCK = 32
CMP_STRIDE = 16
SEL_BLOCK = 64
SEL_TOP_N = 16
N_FORCED = 3
WINDOW = 512
PHI_HIDDEN = 256
NSA_Q_W = NSA_HEADS * HEAD_DIM
NSA_KV_W = NSA_KV_HEADS * HEAD_DIM
DIFF_HEADS = 8
DIFF_QK_DIM = 64
DIFF_V_DIM = 128
DIFF_Q_W = DIFF_HEADS * 2 * DIFF_QK_DIM
DIFF_V_W = DIFF_HEADS * DIFF_V_DIM
ROT_DIM = HEAD_DIM // 4
ROPE_THETA = 500000.0
FFN_DIM = 2752
CONV_WIDTH = 3
EPS = 1e-6
NEG = -1e30
FORCE = 1e4

LANES = 128
SUBLANES = 8
MXU_DIM = 256
VMEM_LIMIT_BYTES = 56 * 1024 * 1024

COL_TILE = MXU_DIM
ROW_TILE = 512
ATT_T = 256
DIFF_HPS = 4
NSA_HPS = 2
V_AUG = SUBLANES
FFN_PAD = 2816
FFN_CHUNK = FFN_PAD // 2
MASK_BIAS = -1e9
NEG_FLOOR = -1e20
REMOVED = -3.0e38

MODE_PLAIN, MODE_BOTH_SCALE, MODE_ROPE, MODE_ROPE_SCALE, MODE_SIGMOID = range(5)
LOG2E = math.log2(math.e)
QK_SCALE = HEAD_DIM ** -0.5 * LOG2E


def _cparams(sem):
    return pltpu.CompilerParams(dimension_semantics=sem, vmem_limit_bytes=VMEM_LIMIT_BYTES)


def _rms(x, g):
    y = x * lax.rsqrt(jnp.mean(x * x, axis=-1, keepdims=True) + EPS)
    return y * g


def _dot(a, b):
    return jnp.dot(a, b, preferred_element_type=jnp.float32)


def _norm_linear_kernel(x_ref, g_ref, w_ref, rope_ref, o_ref, *, modes):
    h = _rms(x_ref[...], g_ref[...]).astype(jnp.bfloat16)
    n_both = sum(mode == MODE_BOTH_SCALE for mode in modes)

    def rope(y):
        cos, s_lo, s_hi = rope_ref[0], rope_ref[1], rope_ref[2]
        parts = []
        for c in range(COL_TILE // LANES):
            yh = y[:, c * LANES:(c + 1) * LANES]
            up = pltpu.roll(yh, LANES - ROT_DIM // 2, 1)
            dn = pltpu.roll(yh, ROT_DIM // 2, 1)
            parts.append(yh * cos + up * s_lo + dn * s_hi)
        return jnp.concatenate(parts, axis=1)

    def store(tile, y):
        o_ref[:, tile * COL_TILE:(tile + 1) * COL_TILE] = y.astype(o_ref.dtype)

    for j, mode in enumerate(modes):
        y = _dot(h, w_ref[:, j * COL_TILE:(j + 1) * COL_TILE])
        if mode == MODE_BOTH_SCALE:
            store(j, y * QK_SCALE)
            store(j + n_both, rope(y) * QK_SCALE)
            continue
        if mode in (MODE_ROPE, MODE_ROPE_SCALE):
            y = rope(y)
        if mode == MODE_ROPE_SCALE:
            y = y * QK_SCALE
        if mode == MODE_SIGMOID:
            y = jax.nn.sigmoid(y)
        store(j + n_both, y)


def norm_linear(x2d, g, w, modes, rope_tab, seq):
    m, d = x2d.shape
    n_both = sum(mode == MODE_BOTH_SCALE for mode in modes)
    assert all(mode == MODE_BOTH_SCALE for mode in modes[:n_both])
    assert w.shape[1] == len(modes) * COL_TILE and m % ROW_TILE == 0 and seq % ROW_TILE == 0
    n = (len(modes) + n_both) * COL_TILE
    tiles_per_seq = seq // ROW_TILE
    return pl.pallas_call(
        functools.partial(_norm_linear_kernel, modes=tuple(modes)),
        out_shape=jax.ShapeDtypeStruct((m, n), jnp.bfloat16),
        grid=(m // ROW_TILE,),
        in_specs=[
            pl.BlockSpec((ROW_TILE, d), lambda i: (i, 0)),
            pl.BlockSpec((1, d), lambda i: (0, 0)),
            pl.BlockSpec(w.shape, lambda i: (0, 0)),
            pl.BlockSpec((3, ROW_TILE, LANES), lambda i: (0, i % tiles_per_seq, 0)),
        ],
        out_specs=pl.BlockSpec((ROW_TILE, n), lambda i: (i, 0)),
        compiler_params=_cparams(("parallel",)),
        name="norm_linear",
    )(x2d, g.reshape(1, d), w, rope_tab)


def _res_linear_kernel(x_ref, a_ref, w_ref, o_ref):
    o_ref[...] = x_ref[...] + _dot(a_ref[...], w_ref[...])


def res_linear(x2d, a2d, w):
    m, d = x2d.shape
    k = a2d.shape[1]
    return pl.pallas_call(
        _res_linear_kernel,
        out_shape=jax.ShapeDtypeStruct((m, d), jnp.float32),
        grid=(m // ROW_TILE,),
        in_specs=[
            pl.BlockSpec((ROW_TILE, d), lambda i: (i, 0)),
            pl.BlockSpec((ROW_TILE, k), lambda i: (i, 0)),
            pl.BlockSpec((k, d), lambda i: (0, 0)),
        ],
        out_specs=pl.BlockSpec((ROW_TILE, d), lambda i: (i, 0)),
        compiler_params=_cparams(("parallel",)),
        name="res_linear",
    )(x2d, a2d, w)


def _compress_kernel(ch_ref, pe_ref, w1_ref, b1_ref, w2_ref, o_ref):
    ch = ch_ref[0, 0, 0]
    w1 = w1_ref[0]
    half = CMP_STRIDE * HEAD_DIM
    nc = ch.shape[0]
    first = _dot(ch, w1[:half])
    second = _dot(ch, w1[half:])
    second = pltpu.roll(second, nc - 1, 0)
    bias = _dot(pe_ref[0], w1)[0:1] + b1_ref[0]
    pre = first + second + bias
    hid = jax.nn.gelu(pre, approximate=True)
    o_ref[0, 0, 0] = _dot(hid.astype(jnp.bfloat16), w2_ref[0]).astype(o_ref.dtype)


def compress(chunks, pe_flat8, w1, b1, w2):
    _, b, hk, nc, cw = chunks.shape
    return pl.pallas_call(
        _compress_kernel,
        out_shape=jax.ShapeDtypeStruct((2, b, hk, nc, HEAD_DIM), jnp.bfloat16),
        grid=(2, b, hk),
        in_specs=[
            pl.BlockSpec((1, 1, 1, nc, cw), lambda s, i, j: (s, i, j, 0, 0)),
            pl.BlockSpec((1, 8, 2 * cw), lambda s, i, j: (s, 0, 0)),
            pl.BlockSpec((1, 2 * cw, PHI_HIDDEN), lambda s, i, j: (s, 0, 0)),
            pl.BlockSpec((1, 1, PHI_HIDDEN), lambda s, i, j: (s, 0, 0)),
            pl.BlockSpec((1, PHI_HIDDEN, HEAD_DIM), lambda s, i, j: (s, 0, 0)),
        ],
        out_specs=pl.BlockSpec((1, 1, 1, nc, HEAD_DIM), lambda s, i, j: (s, i, j, 0, 0)),
        compiler_params=_cparams(("parallel", "parallel", "parallel")),
        name="nsa_compress",
    )(chunks, pe_flat8, w1, b1, w2)


def _softmax_pv(s, vT_tile, m_ref, acc_ref):
    m_old = m_ref[...]
    m_new = jnp.maximum(m_old, jnp.max(s, axis=0, keepdims=True))
    alpha = jnp.exp2(m_old - m_new)
    p = jnp.exp2(s - m_new).astype(jnp.bfloat16)
    acc_ref[...] = alpha * acc_ref[...] + _dot(vT_tile, p)
    m_ref[...] = m_new


def _nsa_attn_kernel(q_ref, qr_ref, kc_ref, vcT_ref, ovT_ref, kaug_ref, vsT_ref, kw_ref, vwT_ref, tri_ref,
                     gate_ref, o_ref, m_s, acc_s, s_buf, *, n_sel, top_n):
    g_heads = NSA_GROUP
    tq = ATT_T
    cols = g_heads * tq
    i = pl.program_id(2)
    qs = i * tq
    heads = range(NSA_HPS)
    def heads_T(ref):
        allT = ref[...].astype(jnp.float32).T
        return [jnp.concatenate([allT[(hh * g_heads + g) * HEAD_DIM:(hh * g_heads + g + 1) * HEAD_DIM]
                                 for g in range(g_heads)], axis=1).astype(jnp.bfloat16) for hh in heads]

    qT = heads_T(q_ref)
    qrT = heads_T(qr_ref)
    t_col = qs + (lax.broadcasted_iota(jnp.int32, (1, cols), 1) & (tq - 1))
    causal_bias = jnp.concatenate([tri_ref[0]] * g_heads, axis=1)
    lower_bias = jnp.concatenate([tri_ref[1]] * g_heads, axis=1)

    nc = kc_ref.shape[2]
    i0 = jnp.maximum(i - 2, 0)
    i1 = jnp.maximum(i - 1, 0)
    sc = [_dot(kc_ref[0, hh], qT[hh]) for hh in heads]
    sw = [(_dot(kw_ref[0, hh, i0], qrT[hh]) + lower_bias,
           _dot(kw_ref[0, hh, i1], qrT[hh]),
           _dot(kw_ref[0, hh, i], qrT[hh]) + causal_bias) for hh in heads]

    c_end = lax.broadcasted_iota(jnp.int32, (nc, 1), 0) * CMP_STRIDE + (CMP_BLOCK - 1)
    ovT = ovT_ref[...]
    o_c, imp = [], []
    for hh in heads:
        s = jnp.where(c_end <= t_col, sc[hh], NEG)
        mc = jnp.maximum(jnp.max(s, axis=0, keepdims=True), NEG_FLOOR)
        ec = jnp.exp2(s - mc)
        lc = jnp.sum(ec, axis=0, keepdims=True)
        pc = ec * jnp.where(lc > 0.0, 1.0 / lc, 0.0)
        o_c.append(_dot(vcT_ref[0, hh], pc.astype(jnp.bfloat16)))
        psum = pc[:, 0:tq]
        for g in range(1, g_heads):
            psum = psum + pc[:, g * tq:(g + 1) * tq]
        p_hi = psum.astype(jnp.bfloat16)
        p_lo = (psum - p_hi.astype(jnp.float32)).astype(jnp.bfloat16)
        imp.append(_dot(ovT, p_hi) + _dot(ovT, p_lo))

    o_w = []
    for hh in heads:
        s0, s1, s2 = sw[hh]
        mw = jnp.maximum(jnp.maximum(s0.max(axis=0, keepdims=True), s1.max(axis=0, keepdims=True)),
                         s2.max(axis=0, keepdims=True))
        r0 = _dot(vwT_ref[0, hh, i0], jnp.exp2(s0 - mw).astype(jnp.bfloat16)) * jnp.where(i >= 2, 1.0, 0.0)
        r1 = _dot(vwT_ref[0, hh, i1], jnp.exp2(s1 - mw).astype(jnp.bfloat16)) * jnp.where(i >= 1, 1.0, 0.0)
        r2 = _dot(vwT_ref[0, hh, i], jnp.exp2(s2 - mw).astype(jnp.bfloat16))
        accw = r0 + r1 + r2
        o_w.append(accw[:HEAD_DIM] * (1.0 / accw[HEAD_DIM:HEAD_DIM + 1]))

    blk = lax.broadcasted_iota(jnp.int32, (n_sel, tq), 0)
    blk_f = blk.astype(jnp.float32)
    cur = (qs + lax.broadcasted_iota(jnp.int32, (1, tq), 1)) >> int(math.log2(SEL_BLOCK))
    work = [jnp.where(blk == 0, NEG, jnp.where(blk >= cur - 1, NEG, imp[hh])) for hh in heads]
    for _ in range(top_n - N_FORCED):
        for hh in heads:
            top = jnp.max(work[hh], axis=0, keepdims=True)
            first = jnp.min(jnp.where(work[hh] == top, blk_f, float(n_sel)), axis=0, keepdims=True)
            work[hh] = jnp.where(blk_f == first, REMOVED, work[hh])
    q_aug = []
    for hh in heads:
        picked = jnp.where(work[hh] == REMOVED, 0.0, MASK_BIAS)
        picked = jnp.where(blk == 0, 0.0, jnp.where(blk >= cur - 1, 0.0, picked))
        selbias = jnp.where(blk <= cur, picked, MASK_BIAS).astype(jnp.bfloat16)
        q_aug.append(jnp.concatenate(
            [qrT[hh], jnp.zeros((LANES - HEAD_DIM, cols), jnp.bfloat16),
             jnp.concatenate([selbias] * g_heads, axis=1)], axis=0))

    m_s[...] = jnp.full(m_s.shape, NEG, jnp.float32)
    acc_s[...] = jnp.zeros(acc_s.shape, jnp.float32)

    for hh in heads:
        s_buf[hh] = _dot(kaug_ref[0, hh, 0], q_aug[hh])

    def body(kt, carry):
        s_cur = [s_buf[hh] for hh in heads]
        for hh in heads:
            s_buf[hh] = _dot(kaug_ref[0, hh, kt + 1], q_aug[hh])
        for hh in heads:
            _softmax_pv(s_cur[hh], vsT_ref[0, hh, kt], m_s.at[hh], acc_s.at[hh])
        return carry

    lax.fori_loop(0, i, body, 0)
    for hh in heads:
        _softmax_pv(s_buf[hh] + causal_bias, vsT_ref[0, hh, i], m_s.at[hh], acc_s.at[hh])
    rows = []
    for hh in heads:
        acc = acc_s[hh]
        o_s = acc[:HEAD_DIM] * (1.0 / acc[HEAD_DIM:HEAD_DIM + 1])
        gate = gate_ref[0, hh, 0]
        out = gate[0:1] * o_c[hh] + gate[1:2] * o_s + gate[2:3] * o_w[hh]
        rows += [out[:, g * tq:(g + 1) * tq] for g in range(g_heads)]
    o_ref[...] = jnp.concatenate(rows, axis=0).T.astype(o_ref.dtype)


def nsa_attention(proj2d, kcmp, vcmpT, ovT, kaug, vsT, kw, vwT, tri, gates, seq):
    b = kcmp.shape[0]
    hk = NSA_KV_HEADS
    qw = NSA_HPS * NSA_GROUP * HEAD_DIM
    nq = seq // ATT_T
    n_sel = seq // SEL_BLOCK
    top_n = min(SEL_TOP_N, n_sel)
    assert top_n > N_FORCED
    nc = kcmp.shape[2]
    cols = NSA_GROUP * ATT_T
    kd = kaug.shape[-1]
    dv = HEAD_DIM + V_AUG
    hps = NSA_HPS
    resident = pl.Buffered(1)
    kern = functools.partial(_nsa_attn_kernel, n_sel=n_sel, top_n=top_n)
    return pl.pallas_call(
        kern,
        out_shape=jax.ShapeDtypeStruct((b * seq, NSA_Q_W), jnp.bfloat16),
        grid=(b, hk // hps, nq),
        in_specs=[
            pl.BlockSpec((ATT_T, qw), lambda bi, h, i: (bi * nq + i, h)),
            pl.BlockSpec((ATT_T, qw), lambda bi, h, i: (bi * nq + i, NSA_Q_W // qw + h)),
            pl.BlockSpec((1, hps, nc, HEAD_DIM), lambda bi, h, i: (bi, h, 0, 0)),
            pl.BlockSpec((1, hps, HEAD_DIM, nc), lambda bi, h, i: (bi, h, 0, 0)),
            pl.BlockSpec((n_sel, nc), lambda bi, h, i: (0, 0)),
            pl.BlockSpec((1, hps, nq, ATT_T, kd), lambda bi, h, i: (bi, h, 0, 0, 0), pipeline_mode=resident),
            pl.BlockSpec((1, hps, nq, dv, ATT_T), lambda bi, h, i: (bi, h, 0, 0, 0), pipeline_mode=resident),
            pl.BlockSpec((1, hps, nq, ATT_T, HEAD_DIM), lambda bi, h, i: (bi, h, 0, 0, 0), pipeline_mode=resident),
            pl.BlockSpec((1, hps, nq, dv, ATT_T), lambda bi, h, i: (bi, h, 0, 0, 0), pipeline_mode=resident),
            pl.BlockSpec((2, ATT_T, ATT_T), lambda bi, h, i: (0, 0, 0)),
            pl.BlockSpec((1, hps, 1, 3, cols), lambda bi, h, i: (bi, h, i, 0, 0)),
        ],
        out_specs=pl.BlockSpec((ATT_T, qw), lambda bi, h, i: (bi * nq + i, h)),
        scratch_shapes=[
            pltpu.VMEM((hps, 1, cols), jnp.float32),
            pltpu.VMEM((hps, dv, cols), jnp.float32),
            pltpu.VMEM((hps, ATT_T, cols), jnp.float32),
        ],
        compiler_params=_cparams(("parallel", "parallel", "arbitrary")),
        name="nsa_attention",
    )(proj2d, proj2d, kcmp, vcmpT, ovT, kaug, vsT, kw, vwT, tri, gates)


def _diff_attn_kernel(q_ref, k_ref, vT_ref, tri_ref, lam_ref, g_ref, o_ref, m_s, acc_s, s_buf, *, lam_init):
    tq = ATT_T
    i = pl.program_id(2)
    zero = jnp.zeros((DIFF_QK_DIM, tq), jnp.bfloat16)
    q_bd = []
    q_allT = q_ref[...].astype(jnp.float32).T.astype(jnp.bfloat16)
    hw = 2 * DIFF_QK_DIM
    for hh in range(DIFF_HPS):
        q2 = q_allT[hh * hw:(hh + 1) * hw]
        q_bd.append(jnp.concatenate(
            [jnp.concatenate([q2[:DIFF_QK_DIM], zero], axis=1),
             jnp.concatenate([zero, q2[DIFF_QK_DIM:]], axis=1)], axis=0))
    causal_bias = jnp.concatenate([tri_ref[0]] * 2, axis=1)

    m_s[...] = jnp.full(m_s.shape, NEG, jnp.float32)
    acc_s[...] = jnp.zeros(acc_s.shape, jnp.float32)

    for hh in range(DIFF_HPS):
        s_buf[hh] = _dot(k_ref[0, hh, 0], q_bd[hh])

    def body(kt, carry):
        s_cur = [s_buf[hh] for hh in range(DIFF_HPS)]
        for hh in range(DIFF_HPS):
            s_buf[hh] = _dot(k_ref[0, hh, kt + 1], q_bd[hh])
        for hh in range(DIFF_HPS):
            _softmax_pv(s_cur[hh], vT_ref[0, hh, kt], m_s.at[hh], acc_s.at[hh])
        return carry

    lax.fori_loop(0, i, body, 0)
    for hh in range(DIFF_HPS):
        _softmax_pv(s_buf[hh] + causal_bias, vT_ref[0, hh, i], m_s.at[hh], acc_s.at[hh])

    lv = lam_ref[...]
    lam = (jnp.exp(jnp.sum(lv[0:1] * lv[1:2], axis=1, keepdims=True))
           - jnp.exp(jnp.sum(lv[2:3] * lv[3:4], axis=1, keepdims=True)) + lam_init)
    g = jnp.concatenate([g_ref[...]] * (tq // LANES), axis=1)
    rows = []
    for hh in range(DIFF_HPS):
        acc = acc_s[hh]
        o = acc[:DIFF_V_DIM] * (1.0 / acc[DIFF_V_DIM:DIFF_V_DIM + 1])
        a = o[:, :tq] - lam * o[:, tq:]
        y = a * lax.rsqrt(jnp.mean(a * a, axis=0, keepdims=True) + EPS)
        rows.append((y * g) * (1.0 - lam_init))
    o_ref[...] = jnp.concatenate(rows, axis=0).T.astype(o_ref.dtype)


def diff_attention(q2d, kcat, vT, tri, lam_vec, g_rows, lam_init, seq):
    b = kcat.shape[0]
    nq = seq // ATT_T
    cols = 2 * ATT_T
    dv = DIFF_V_DIM + V_AUG
    hps = DIFF_HPS
    return pl.pallas_call(
        functools.partial(_diff_attn_kernel, lam_init=lam_init),
        out_shape=jax.ShapeDtypeStruct((b * seq, DIFF_V_W), jnp.bfloat16),
        grid=(b, DIFF_HEADS // hps, nq),
        in_specs=[
            pl.BlockSpec((ATT_T, hps * 2 * DIFF_QK_DIM), lambda bi, h, i: (bi * nq + i, h)),
            pl.BlockSpec((1, hps, nq, ATT_T, 2 * DIFF_QK_DIM), lambda bi, h, i: (bi, h, 0, 0, 0),
                         pipeline_mode=pl.Buffered(1)),
            pl.BlockSpec((1, hps, nq, dv, ATT_T), lambda bi, h, i: (bi, h, 0, 0, 0),
                         pipeline_mode=pl.Buffered(1)),
            pl.BlockSpec((2, ATT_T, ATT_T), lambda bi, h, i: (0, 0, 0)),
            pl.BlockSpec((4, DIFF_QK_DIM), lambda bi, h, i: (0, 0)),
            pl.BlockSpec((DIFF_V_DIM, LANES), lambda bi, h, i: (0, 0)),
        ],
        out_specs=pl.BlockSpec((ATT_T, hps * DIFF_V_DIM), lambda bi, h, i: (bi * nq + i, h)),
        scratch_shapes=[
            pltpu.VMEM((hps, 1, cols), jnp.float32),
            pltpu.VMEM((hps, dv, cols), jnp.float32),
            pltpu.VMEM((hps, ATT_T, cols), jnp.float32),
        ],
        compiler_params=_cparams(("parallel", "parallel", "arbitrary")),
        name="diff_attention",
    )(q2d, kcat, vT, tri, lam_vec, g_rows)


HALO = SUBLANES


def _ffn_kernel(x_ref, xh_ref, g_ref, wg_ref, wv_ref, cwg_ref, cwv_ref, cbg_ref, cbv_ref, wo_ref, fg_ref,
                o_ref, h_s, acc_s, *, tiles_per_seq, final_norm):
    i = pl.program_id(0)
    f = pl.program_id(1)

    @pl.when(f == 0)
    def _():
        g = g_ref[...]
        h_s[HALO:, :] = _rms(x_ref[...], g).astype(jnp.bfloat16)
        hh = _rms(xh_ref[...], g)
        hh = jnp.where(i % tiles_per_seq == 0, 0.0, hh)
        h_s[:HALO, :] = hh.astype(jnp.bfloat16)
        acc_s[...] = jnp.zeros(acc_s.shape, jnp.float32)

    h = h_s[...]

    def conv(u, cw_ref, cb_ref):
        cw = cw_ref[...]
        c = cb_ref[...] + u[HALO:] * cw[2:3]
        c = c + pltpu.roll(u, 1, 0)[HALO:] * cw[1:2]
        c = c + pltpu.roll(u, 2, 0)[HALO:] * cw[0:1]
        return c

    gate = conv(_dot(h, wg_ref[...]), cwg_ref, cbg_ref)
    val = conv(_dot(h, wv_ref[...]), cwv_ref, cbv_ref)
    act = (gate * jax.nn.sigmoid(gate)) * val
    acc_s[...] += _dot(act.astype(jnp.bfloat16), wo_ref[...])

    @pl.when(f == pl.num_programs(1) - 1)
    def _():
        y = x_ref[...] + acc_s[...]
        if final_norm:
            y = _rms(y, fg_ref[...])
        o_ref[...] = y


def ffn(x2d, g, wg, wv, cwg, cwv, cbg, cbv, wo, final_g, seq):
    m, d = x2d.shape
    tm = ROW_TILE
    nf = FFN_PAD // FFN_CHUNK
    tiles_per_seq = seq // tm
    final_norm = final_g is not None
    fg = (final_g if final_norm else g).reshape(1, d)
    kern = functools.partial(_ffn_kernel, tiles_per_seq=tiles_per_seq, final_norm=final_norm)
    return pl.pallas_call(
        kern,
        out_shape=jax.ShapeDtypeStruct((m, d), jnp.float32),
        grid=(m // tm, nf),
        in_specs=[
            pl.BlockSpec((tm, d), lambda i, f: (i, 0)),
            pl.BlockSpec((HALO, d), lambda i, f: (jnp.maximum(i * (tm // HALO) - 1, 0), 0)),
            pl.BlockSpec((1, d), lambda i, f: (0, 0)),
            pl.BlockSpec((d, FFN_CHUNK), lambda i, f: (0, f)),
            pl.BlockSpec((d, FFN_CHUNK), lambda i, f: (0, f)),
            pl.BlockSpec((CONV_WIDTH, FFN_CHUNK), lambda i, f: (0, f)),
            pl.BlockSpec((CONV_WIDTH, FFN_CHUNK), lambda i, f: (0, f)),
            pl.BlockSpec((1, FFN_CHUNK), lambda i, f: (0, f)),
            pl.BlockSpec((1, FFN_CHUNK), lambda i, f: (0, f)),
            pl.BlockSpec((FFN_CHUNK, d), lambda i, f: (f, 0)),
            pl.BlockSpec((1, d), lambda i, f: (0, 0)),
        ],
        out_specs=pl.BlockSpec((tm, d), lambda i, f: (i, 0)),
        scratch_shapes=[
            pltpu.VMEM((HALO + tm, d), jnp.bfloat16),
            pltpu.VMEM((tm, d), jnp.float32),
        ],
        compiler_params=_cparams(("parallel", "arbitrary")),
        name="conv_glu_ffn",
    )(x2d, x2d, g.reshape(1, d), wg, wv, cwg, cwv, cbg, cbv, wo, fg)


def _rope_tables(seq):
    half = ROT_DIM // 2
    inv = ROPE_THETA ** (-jnp.arange(half, dtype=jnp.float32) / half)
    ang = jnp.arange(seq).astype(jnp.float32)[:, None] * inv[None, :]
    cos, sin = jnp.cos(ang), jnp.sin(ang)
    ones = jnp.ones((seq, HEAD_DIM - ROT_DIM), jnp.float32)
    zeros = jnp.zeros((seq, HEAD_DIM - ROT_DIM), jnp.float32)
    z8 = jnp.zeros((seq, half), jnp.float32)
    c = jnp.concatenate([cos, cos, ones], axis=1)
    s_lo = jnp.concatenate([-sin, z8, zeros], axis=1)
    s_hi = jnp.concatenate([z8, sin, zeros], axis=1)
    tab = jnp.stack([c, s_lo, s_hi])
    return jnp.concatenate([tab] * (LANES // HEAD_DIM), axis=2)


def _triangle_biases():
    rk = jnp.arange(ATT_T)[:, None]
    rq = jnp.arange(ATT_T)[None, :]
    causal = jnp.where(rk <= rq, 0.0, MASK_BIAS)
    lower = jnp.where(rk > rq, 0.0, MASK_BIAS)
    return jnp.stack([causal, lower]).astype(jnp.float32)


def _pad_cols(w, n):
    return jnp.pad(w, ((0, 0), (0, n - w.shape[1])))


def _vT_aug(v, bsz, seq, heads, dv):
    nt = seq // ATT_T
    vT = v.reshape(bsz, nt, ATT_T, heads, dv).transpose(0, 3, 1, 4, 2)
    extra = jnp.zeros((bsz, heads, nt, V_AUG, ATT_T), v.dtype).at[:, :, :, 0, :].set(1.0)
    return jnp.concatenate([vT, extra], axis=3)


def _nsa_layer(x2d, bsz, seq, g, w_in, w_out, pe, w1, b1, w2, rope_tab, tri):
    bf = jnp.bfloat16
    hk, hd = NSA_KV_HEADS, HEAD_DIM
    wq = w_in[:, :NSA_Q_W]
    wkv = [w_in[:, NSA_Q_W + k * NSA_KV_W:NSA_Q_W + (k + 1) * NSA_KV_W] for k in range(6)]
    wg = _pad_cols(w_in[:, NSA_Q_W + 6 * NSA_KV_W:], COL_TILE)
    w_all = jnp.concatenate([wq] + wkv + [wg], axis=1).astype(bf)
    nq_t = NSA_Q_W // COL_TILE
    modes = ([MODE_BOTH_SCALE] * nq_t
             + [MODE_PLAIN, MODE_PLAIN, MODE_ROPE, MODE_PLAIN, MODE_ROPE, MODE_PLAIN, MODE_SIGMOID])
    proj2d = norm_linear(x2d, g, w_all, modes, rope_tab, seq)
    proj = proj2d.reshape(bsz, seq, -1)

    off = 2 * NSA_Q_W
    kc, vc, ks, vs, kw, vw = (proj[..., off + k * NSA_KV_W:off + (k + 1) * NSA_KV_W] for k in range(6))
    off += 6 * NSA_KV_W
    gates = proj[..., off:off + 3 * NSA_HEADS]

    nch = seq // CMP_STRIDE
    def chunked(a):
        return a.reshape(bsz, nch, CMP_STRIDE, hk, hd).transpose(0, 3, 1, 2, 4).reshape(bsz, hk, nch, CMP_STRIDE * hd)
    chunks = jnp.stack([chunked(kc), chunked(vc)])
    pe_flat8 = jnp.broadcast_to(pe.reshape(2, 1, CMP_BLOCK * hd), (2, 8, CMP_BLOCK * hd)).astype(bf)
    cmp = compress(chunks, pe_flat8, w1.astype(bf), b1.reshape(2, 1, PHI_HIDDEN), w2.astype(bf))
    kcmp = cmp[0]
    vcmpT = cmp[1].transpose(0, 1, 3, 2)

    n_sel = seq // SEL_BLOCK
    ci = jnp.arange(nch)[None, :]
    sj = jnp.arange(n_sel)[:, None]
    ovT = ((ci * CMP_STRIDE < (sj + 1) * SEL_BLOCK) & (ci * CMP_STRIDE + CMP_BLOCK > sj * SEL_BLOCK)
           & (ci < nch - 1)).astype(bf)

    nt = seq // ATT_T
    ks_h = ks.reshape(bsz, seq, hk, hd).transpose(0, 2, 1, 3)
    onehot = (jnp.arange(seq)[:, None] // SEL_BLOCK == jnp.arange(n_sel)[None, :]).astype(bf)
    kaug = jnp.concatenate(
        [ks_h, jnp.zeros((bsz, hk, seq, LANES - hd), bf),
         jnp.broadcast_to(onehot, (bsz, hk, seq, n_sel))], axis=-1)
    kaug = kaug.reshape(bsz, hk, nt, ATT_T, LANES + n_sel)
    vsT = _vT_aug(vs, bsz, seq, hk, hd)
    kw_t = kw.reshape(bsz, nt, ATT_T, hk, hd).transpose(0, 3, 1, 2, 4)
    vwT = _vT_aug(vw, bsz, seq, hk, hd)
    gates_t = gates.reshape(bsz, nt, ATT_T, hk, NSA_GROUP, 3).transpose(0, 3, 1, 5, 4, 2)
    gates_t = gates_t.reshape(bsz, hk, nt, 3, NSA_GROUP * ATT_T).astype(jnp.float32)

    o = nsa_attention(proj2d, kcmp, vcmpT, ovT, kaug, vsT, kw_t, vwT, tri, gates_t, seq)
    return res_linear(x2d, o, w_out.astype(bf))


def _shared_kv(x2d, bsz, seq, g, kv_w, rope_tab):
    nk = DIFF_Q_W // COL_TILE
    nv = DIFF_V_W // COL_TILE
    kv = norm_linear(x2d, g, kv_w.astype(jnp.bfloat16), [MODE_ROPE] * nk + [MODE_PLAIN] * nv, rope_tab, seq)
    kv = kv.reshape(bsz, seq, -1)
    k = kv[..., :DIFF_Q_W].reshape(bsz, seq // ATT_T, ATT_T, DIFF_HEADS, 2 * DIFF_QK_DIM)
    kcat = k.transpose(0, 3, 1, 2, 4)
    vT = _vT_aug(kv[..., DIFF_Q_W:], bsz, seq, DIFF_HEADS, DIFF_V_DIM)
    return kcat, vT


def _diff_layer(x2d, bsz, seq, g, w_q, lam_vec, subln_g, w_out, kcat, vT, lam_init, rope_tab, tri):
    bf = jnp.bfloat16
    q = norm_linear(x2d, g, w_q.astype(bf), [MODE_ROPE_SCALE] * (DIFF_Q_W // COL_TILE), rope_tab, seq)
    g_rows = jnp.broadcast_to(subln_g.astype(jnp.float32)[:, None], (DIFF_V_DIM, LANES))
    o = diff_attention(q, kcat, vT, tri, lam_vec.astype(jnp.float32), g_rows, lam_init, seq)
    return res_linear(x2d, o, w_out.astype(bf))


def _ffn_layer(x2d, seq, g, w_in, conv_w, conv_b, w_out, final_g):
    bf = jnp.bfloat16
    f = FFN_DIM
    wg = _pad_cols(w_in[:, :f], FFN_PAD).astype(bf)
    wv = _pad_cols(w_in[:, f:], FFN_PAD).astype(bf)
    cwg = _pad_cols(conv_w[:, :f], FFN_PAD)
    cwv = _pad_cols(conv_w[:, f:], FFN_PAD)
    cbg = _pad_cols(conv_b[None, :f], FFN_PAD)
    cbv = _pad_cols(conv_b[None, f:], FFN_PAD)
    wo = jnp.pad(w_out, ((0, FFN_PAD - f), (0, 0))).astype(bf)
    return ffn(x2d, g, wg, wv, cwg, cwv, cbg, cbv, wo, final_g, seq)


def kernel(x, attn_norm_g, ffn_norm_g, nsa_w_in, nsa_w_out, nsa_phi_pe, nsa_phi_w1, nsa_phi_b1, nsa_phi_w2,
           kv_norm_g, kv_w, diff_w_q, diff_lambda, diff_subln_g, diff_w_out, ffn_w_in, ffn_conv_w, ffn_conv_b,
           ffn_w_out, final_norm_g):
    bsz, seq, d = x.shape
    x2d = x.reshape(bsz * seq, d)
    rope_tab = _rope_tables(seq)
    tri = _triangle_biases()
    kcat = vT = None
    for l in range(DEPTH):
        if l < N_A_LAYERS:
            x2d = _nsa_layer(x2d, bsz, seq, attn_norm_g[l], nsa_w_in[l], nsa_w_out[l], nsa_phi_pe[l],
                             nsa_phi_w1[l], nsa_phi_b1[l], nsa_phi_w2[l], rope_tab, tri)
        else:
            j = l - N_A_LAYERS
            lam_init = 0.8 - 0.6 * math.exp(-0.3 * l)
            x2d = _diff_layer(x2d, bsz, seq, attn_norm_g[l], diff_w_q[j], diff_lambda[j], diff_subln_g[j],
                              diff_w_out[j], kcat, vT, lam_init, rope_tab, tri)
        final_g = final_norm_g if l == DEPTH - 1 else None
        x2d = _ffn_layer(x2d, seq, ffn_norm_g[l], ffn_w_in[l], ffn_conv_w[l], ffn_conv_b[l], ffn_w_out[l], final_g)
        if l == N_A_LAYERS - 1:
            kcat, vT = _shared_kv(x2d, bsz, seq, kv_norm_g, kv_w, rope_tab)
    return x2d.reshape(bsz, seq, d)
```

```python
import functools
import math

import jax
import jax.numpy as jnp
from jax import lax
from jax.experimental import pallas as pl
from jax.experimental.pallas import tpu as pltpu

D_MODEL = 1024
DEPTH = 4
N_A_LAYERS = DEPTH // 2
NSA_HEADS = 16
NSA_KV_HEADS = 4
NSA_GROUP = NSA_HEADS // NSA_KV_HEADS
HEAD_DIM = 64
CMP_BLOCK = 32
CMP_STRIDE = 16
SEL_BLOCK = 64
SEL_TOP_N = 16
N_FORCED = 3
WINDOW = 512
PHI_HIDDEN = 256
NSA_Q_W = NSA_HEADS * HEAD_DIM
NSA_KV_W = NSA_KV_HEADS * HEAD_DIM
DIFF_HEADS = 8
DIFF_QK_DIM = 64
DIFF_V_DIM = 128
DIFF_Q_W = DIFF_HEADS * 2 * DIFF_QK_DIM
DIFF_V_W = DIFF_HEADS * DIFF_V_DIM
ROT_DIM = HEAD_DIM // 4
ROPE_THETA = 500000.0
FFN_DIM = 2752
CONV_WIDTH = 3
EPS = 1e-6
NEG = -1e30
FORCE = 1e4

LANES = 128
SUBLANES = 8
MXU_DIM = 256
VMEM_LIMIT_BYTES = 56 * 1024 * 1024

COL_TILE = MXU_DIM
ROW_TILE = 512
ATT_T = 256
DIFF_HPS = 4
NSA_HPS = 2
V_AUG = SUBLANES
FFN_PAD = 2816
FFN_CHUNK = FFN_PAD // 2
MASK_BIAS = -1e9
NEG_FLOOR = -1e20
REMOVED = -3.0e38

MODE_PLAIN, MODE_BOTH_SCALE, MODE_ROPE, MODE_ROPE_SCALE, MODE_SIGMOID = range(5)
LOG2E = math.log2(math.e)
QK_SCALE = HEAD_DIM ** -0.5 * LOG2E


def _cparams(sem):
    return pltpu.CompilerParams(dimension_semantics=sem, vmem_limit_bytes=VMEM_LIMIT_BYTES)


def _rms(x, g):
    y = x * lax.rsqrt(jnp.mean(x * x, axis=-1, keepdims=True) + EPS)
    return y * g


def _dot(a, b):
    return jnp.dot(a, b, preferred_element_type=jnp.float32)


def _norm_linear_kernel(x_ref, g_ref, w_ref, rope_ref, o_ref, *, modes):
    h = _rms(x_ref[...], g_ref[...]).astype(jnp.bfloat16)
    n_both = sum(mode == MODE_BOTH_SCALE for mode in modes)

    def rope(y):
        cos, s_lo, s_hi = rope_ref[0], rope_ref[1], rope_ref[2]
        parts = []
        for c in range(COL_TILE // LANES):
            yh = y[:, c * LANES:(c + 1) * LANES]
            up = pltpu.roll(yh, LANES - ROT_DIM // 2, 1)
            dn = pltpu.roll(yh, ROT_DIM // 2, 1)
            parts.append(yh * cos + up * s_lo + dn * s_hi)
        return jnp.concatenate(parts, axis=1)

    def store(tile, y):
        o_ref[:, tile * COL_TILE:(tile + 1) * COL_TILE] = y.astype(o_ref.dtype)

    for j, mode in enumerate(modes):
        y = _dot(h, w_ref[:, j * COL_TILE:(j + 1) * COL_TILE])
        if mode == MODE_BOTH_SCALE:
            store(j, y * QK_SCALE)
            store(j + n_both, rope(y) * QK_SCALE)
            continue
        if mode in (MODE_ROPE, MODE_ROPE_SCALE):
            y = rope(y)
        if mode == MODE_ROPE_SCALE:
            y = y * QK_SCALE
        if mode == MODE_SIGMOID:
            y = jax.nn.sigmoid(y)
        store(j + n_both, y)


def norm_linear(x2d, g, w, modes, rope_tab, seq):
    m, d = x2d.shape
    n_both = sum(mode == MODE_BOTH_SCALE for mode in modes)
    assert all(mode == MODE_BOTH_SCALE for mode in modes[:n_both])
    assert w.shape[1] == len(modes) * COL_TILE and m % ROW_TILE == 0 and seq % ROW_TILE == 0
    n = (len(modes) + n_both) * COL_TILE
    tiles_per_seq = seq // ROW_TILE
    return pl.pallas_call(
        functools.partial(_norm_linear_kernel, modes=tuple(modes)),
        out_shape=jax.ShapeDtypeStruct((m, n), jnp.bfloat16),
        grid=(m // ROW_TILE,),
        in_specs=[
            pl.BlockSpec((ROW_TILE, d), lambda i: (i, 0)),
            pl.BlockSpec((1, d), lambda i: (0, 0)),
            pl.BlockSpec(w.shape, lambda i: (0, 0)),
            pl.BlockSpec((3, ROW_TILE, LANES), lambda i: (0, i % tiles_per_seq, 0)),
        ],
        out_specs=pl.BlockSpec((ROW_TILE, n), lambda i: (i, 0)),
        compiler_params=_cparams(("parallel",)),
        name="norm_linear",
    )(x2d, g.reshape(1, d), w, rope_tab)


def _compress_kernel(ch_ref, pe_ref, w1_ref, b1_ref, w2_ref, o_ref):
    ch = ch_ref[0, 0, 0]
    w1 = w1_ref[0]
    half = CMP_STRIDE * HEAD_DIM
    nc = ch.shape[0]
    first = _dot(ch, w1[:half])
    second = _dot(ch, w1[half:])
    second = pltpu.roll(second, nc - 1, 0)
    bias = _dot(pe_ref[0], w1)[0:1] + b1_ref[0]
    pre = first + second + bias
    hid = jax.nn.gelu(pre, approximate=True)
    o_ref[0, 0, 0] = _dot(hid.astype(jnp.bfloat16), w2_ref[0]).astype(o_ref.dtype)


def compress(chunks, pe_flat8, w1, b1, w2):
    _, b, hk, nc, cw = chunks.shape
    return pl.pallas_call(
        _compress_kernel,
        out_shape=jax.ShapeDtypeStruct((2, b, hk, nc, HEAD_DIM), jnp.bfloat16),
        grid=(2, b, hk),
        in_specs=[
            pl.BlockSpec((1, 1, 1, nc, cw), lambda s, i, j: (s, i, j, 0, 0)),
            pl.BlockSpec((1, 8, 2 * cw), lambda s, i, j: (s, 0, 0)),
            pl.BlockSpec((1, 2 * cw, PHI_HIDDEN), lambda s, i, j: (s, 0, 0)),
            pl.BlockSpec((1, 1, PHI_HIDDEN), lambda s, i, j: (s, 0, 0)),
            pl.BlockSpec((1, PHI_HIDDEN, HEAD_DIM), lambda s, i, j: (s, 0, 0)),
        ],
        out_specs=pl.BlockSpec((1, 1, 1, nc, HEAD_DIM), lambda s, i, j: (s, i, j, 0, 0)),
        compiler_params=_cparams(("parallel", "parallel", "parallel")),
        name="nsa_compress",
    )(chunks, pe_flat8, w1, b1, w2)


def _softmax_pv(s, vT_tile, m_ref, acc_ref):
    m_old = m_ref[...]
    m_new = jnp.maximum(m_old, jnp.max(s, axis=0, keepdims=True))
    alpha = jnp.exp2(m_old - m_new)
    p = jnp.exp2(s - m_new).astype(jnp.bfloat16)
    acc_ref[...] = alpha * acc_ref[...] + _dot(vT_tile, p)
    m_ref[...] = m_new


def _nsa_attn_kernel(q_ref, qr_ref, kc_ref, vcT_ref, ovT_ref, kaug_ref, vsT_ref, kw_ref, vwT_ref, tri_ref,
                     gate_ref, o_ref, m_s, acc_s, s_buf, *, n_sel, top_n):
    g_heads = NSA_GROUP
    tq = ATT_T
    cols = g_heads * tq
    i = pl.program_id(2)
    qs = i * tq
    heads = range(NSA_HPS)
    def heads_T(ref):
        allT = ref[...].astype(jnp.float32).T
        return [jnp.concatenate([allT[(hh * g_heads + g) * HEAD_DIM:(hh * g_heads + g + 1) * HEAD_DIM]
                                 for g in range(g_heads)], axis=1).astype(jnp.bfloat16) for hh in heads]

    qT = heads_T(q_ref)
    qrT = heads_T(qr_ref)
    t_col = qs + (lax.broadcasted_iota(jnp.int32, (1, cols), 1) & (tq - 1))
    causal_bias = jnp.concatenate([tri_ref[0]] * g_heads, axis=1)
    lower_bias = jnp.concatenate([tri_ref[1]] * g_heads, axis=1)

    nc = kc_ref.shape[2]
    i0 = jnp.maximum(i - 2, 0)
    i1 = jnp.maximum(i - 1, 0)
    sc = [_dot(kc_ref[0, hh], qT[hh]) for hh in heads]
    sw = [(_dot(kw_ref[0, hh, i0], qrT[hh]) + lower_bias,
           _dot(kw_ref[0, hh, i1], qrT[hh]),
           _dot(kw_ref[0, hh, i], qrT[hh]) + causal_bias) for hh in heads]

    c_end = lax.broadcasted_iota(jnp.int32, (nc, 1), 0) * CMP_STRIDE + (CMP_BLOCK - 1)
    ovT = ovT_ref[...]
    o_c, imp = [], []
    for hh in heads:
        s = jnp.where(c_end <= t_col, sc[hh], NEG)
        mc = jnp.maximum(jnp.max(s, axis=0, keepdims=True), NEG_FLOOR)
        ec = jnp.exp2(s - mc)
        lc = jnp.sum(ec, axis=0, keepdims=True)
        pc = ec * jnp.where(lc > 0.0, 1.0 / lc, 0.0)
        o_c.append(_dot(vcT_ref[0, hh], pc.astype(jnp.bfloat16)))
        psum = pc[:, 0:tq]
        for g in range(1, g_heads):
            psum = psum + pc[:, g * tq:(g + 1) * tq]
        p_hi = psum.astype(jnp.bfloat16)
        p_lo = (psum - p_hi.astype(jnp.float32)).astype(jnp.bfloat16)
        imp.append(_dot(ovT, p_hi) + _dot(ovT, p_lo))

    o_w = []
    for hh in heads:
        s0, s1, s2 = sw[hh]
        mw = jnp.maximum(jnp.maximum(s0.max(axis=0, keepdims=True), s1.max(axis=0, keepdims=True)),
                         s2.max(axis=0, keepdims=True))
        r0 = _dot(vwT_ref[0, hh, i0], jnp.exp2(s0 - mw).astype(jnp.bfloat16)) * jnp.where(i >= 2, 1.0, 0.0)
        r1 = _dot(vwT_ref[0, hh, i1], jnp.exp2(s1 - mw).astype(jnp.bfloat16)) * jnp.where(i >= 1, 1.0, 0.0)
        r2 = _dot(vwT_ref[0, hh, i], jnp.exp2(s2 - mw).astype(jnp.bfloat16))
        accw = r0 + r1 + r2
        o_w.append(accw[:HEAD_DIM] * (1.0 / accw[HEAD_DIM:HEAD_DIM + 1]))

    blk = lax.broadcasted_iota(jnp.int32, (n_sel, tq), 0)
    blk_f = blk.astype(jnp.float32)
    cur = (qs + lax.broadcasted_iota(jnp.int32, (1, tq), 1)) >> int(math.log2(SEL_BLOCK))
    work = [jnp.where(blk == 0, NEG, jnp.where(blk >= cur - 1, NEG, imp[hh])) for hh in heads]
    for _ in range(top_n - N_FORCED):
        for hh in heads:
            top = jnp.max(work[hh], axis=0, keepdims=True)
            first = jnp.min(jnp.where(work[hh] == top, blk_f, float(n_sel)), axis=0, keepdims=True)
            work[hh] = jnp.where(blk_f == first, REMOVED, work[hh])
    q_aug = []
    for hh in heads:
        picked = jnp.where(work[hh] == REMOVED, 0.0, MASK_BIAS)
        picked = jnp.where(blk == 0, 0.0, jnp.where(blk >= cur - 1, 0.0, picked))
        selbias = jnp.where(blk <= cur, picked, MASK_BIAS).astype(jnp.bfloat16)
        q_aug.append(jnp.concatenate(
            [qrT[hh], jnp.zeros((LANES - HEAD_DIM, cols), jnp.bfloat16),
             jnp.concatenate([selbias] * g_heads, axis=1)], axis=0))

    m_s[...] = jnp.full(m_s.shape, NEG, jnp.float32)
    acc_s[...] = jnp.zeros(acc_s.shape, jnp.float32)

    for hh in heads:
        s_buf[hh] = _dot(kaug_ref[0, hh, 0], q_aug[hh])

    def body(kt, carry):
        s_cur = [s_buf[hh] for hh in heads]
        for hh in heads:
            s_buf[hh] = _dot(kaug_ref[0, hh, kt + 1], q_aug[hh])
        for hh in heads:
            _softmax_pv(s_cur[hh], vsT_ref[0, hh, kt], m_s.at[hh], acc_s.at[hh])
        return carry

    lax.fori_loop(0, i, body, 0)
    for hh in heads:
        _softmax_pv(s_buf[hh] + causal_bias, vsT_ref[0, hh, i], m_s.at[hh], acc_s.at[hh])
    rows = []
    for hh in heads:
        acc = acc_s[hh]
        o_s = acc[:HEAD_DIM] * (1.0 / acc[HEAD_DIM:HEAD_DIM + 1])
        gate = gate_ref[0, hh, 0]
        out = gate[0:1] * o_c[hh] + gate[1:2] * o_s + gate[2:3] * o_w[hh]
        rows += [out[:, g * tq:(g + 1) * tq] for g in range(g_heads)]
    o_ref[...] = jnp.concatenate(rows, axis=0).T.astype(o_ref.dtype)


def nsa_attention(proj2d, kcmp, vcmpT, ovT, kaug, vsT, kw, vwT, tri, gates, seq):
    b = kcmp.shape[0]
    hk = NSA_KV_HEADS
    qw = NSA_HPS * NSA_GROUP * HEAD_DIM
    nq = seq // ATT_T
    n_sel = seq // SEL_BLOCK
    top_n = min(SEL_TOP_N, n_sel)
    assert top_n > N_FORCED
    nc = kcmp.shape[2]
    cols = NSA_GROUP * ATT_T
    kd = kaug.shape[-1]
    dv = HEAD_DIM + V_AUG
    hps = NSA_HPS
    resident = pl.Buffered(1)
    kern = functools.partial(_nsa_attn_kernel, n_sel=n_sel, top_n=top_n)
    return pl.pallas_call(
        kern,
        out_shape=jax.ShapeDtypeStruct((b * seq, NSA_Q_W), jnp.bfloat16),
        grid=(b, hk // hps, nq),
        in_specs=[
            pl.BlockSpec((ATT_T, qw), lambda bi, h, i: (bi * nq + i, h)),
            pl.BlockSpec((ATT_T, qw), lambda bi, h, i: (bi * nq + i, NSA_Q_W // qw + h)),
            pl.BlockSpec((1, hps, nc, HEAD_DIM), lambda bi, h, i: (bi, h, 0, 0)),
            pl.BlockSpec((1, hps, HEAD_DIM, nc), lambda bi, h, i: (bi, h, 0, 0)),
            pl.BlockSpec((n_sel, nc), lambda bi, h, i: (0, 0)),
            pl.BlockSpec((1, hps, nq, ATT_T, kd), lambda bi, h, i: (bi, h, 0, 0, 0), pipeline_mode=resident),
            pl.BlockSpec((1, hps, nq, dv, ATT_T), lambda bi, h, i: (bi, h, 0, 0, 0), pipeline_mode=resident),
            pl.BlockSpec((1, hps, nq, ATT_T, HEAD_DIM), lambda bi, h, i: (bi, h, 0, 0, 0), pipeline_mode=resident),
            pl.BlockSpec((1, hps, nq, dv, ATT_T), lambda bi, h, i: (bi, h, 0, 0, 0), pipeline_mode=resident),
            pl.BlockSpec((2, ATT_T, ATT_T), lambda bi, h, i: (0, 0, 0)),
            pl.BlockSpec((1, hps, 1, 3, cols), lambda bi, h, i: (bi, h, i, 0, 0)),
        ],
        out_specs=pl.BlockSpec((ATT_T, qw), lambda bi, h, i: (bi * nq + i, h)),
        scratch_shapes=[
            pltpu.VMEM((hps, 1, cols), jnp.float32),
            pltpu.VMEM((hps, dv, cols), jnp.float32),
            pltpu.VMEM((hps, ATT_T, cols), jnp.float32),
        ],
        compiler_params=_cparams(("parallel", "parallel", "arbitrary")),
        name="nsa_attention",
    )(proj2d, proj2d, kcmp, vcmpT, ovT, kaug, vsT, kw, vwT, tri, gates)


def _diff_attn_kernel(q_ref, k_ref, vT_ref, tri_ref, lam_ref, g_ref, o_ref, m_s, acc_s, s_buf, *, lam_init):
    tq = ATT_T
    i = pl.program_id(2)
    zero = jnp.zeros((DIFF_QK_DIM, tq), jnp.bfloat16)
    q_bd = []
    q_allT = q_ref[...].astype(jnp.float32).T.astype(jnp.bfloat16)
    hw = 2 * DIFF_QK_DIM
    for hh in range(DIFF_HPS):
        q2 = q_allT[hh * hw:(hh + 1) * hw]
        q_bd.append(jnp.concatenate(
            [jnp.concatenate([q2[:DIFF_QK_DIM], zero], axis=1),
             jnp.concatenate([zero, q2[DIFF_QK_DIM:]], axis=1)], axis=0))
    causal_bias = jnp.concatenate([tri_ref[0]] * 2, axis=1)

    m_s[...] = jnp.full(m_s.shape, NEG, jnp.float32)
    acc_s[...] = jnp.zeros(acc_s.shape, jnp.float32)

    for hh in range(DIFF_HPS):
        s_buf[hh] = _dot(k_ref[0, hh, 0], q_bd[hh])

    def body(kt, carry):
        s_cur = [s_buf[hh] for hh in range(DIFF_HPS)]
        for hh in range(DIFF_HPS):
            s_buf[hh] = _dot(k_ref[0, hh, kt + 1], q_bd[hh])
        for hh in range(DIFF_HPS):
            _softmax_pv(s_cur[hh], vT_ref[0, hh, kt], m_s.at[hh], acc_s.at[hh])
        return carry

    lax.fori_loop(0, i, body, 0)
    for hh in range(DIFF_HPS):
        _softmax_pv(s_buf[hh] + causal_bias, vT_ref[0, hh, i], m_s.at[hh], acc_s.at[hh])

    lv = lam_ref[...]
    lam = (jnp.exp(jnp.sum(lv[0:1] * lv[1:2], axis=1, keepdims=True))
           - jnp.exp(jnp.sum(lv[2:3] * lv[3:4], axis=1, keepdims=True)) + lam_init)
    g = jnp.concatenate([g_ref[...]] * (tq // LANES), axis=1)
    rows = []
    for hh in range(DIFF_HPS):
        acc = acc_s[hh]
        o = acc[:DIFF_V_DIM] * (1.0 / acc[DIFF_V_DIM:DIFF_V_DIM + 1])
        a = o[:, :tq] - lam * o[:, tq:]
        y = a * lax.rsqrt(jnp.mean(a * a, axis=0, keepdims=True) + EPS)
        rows.append((y * g) * (1.0 - lam_init))
    o_ref[...] = jnp.concatenate(rows, axis=0).T.astype(o_ref.dtype)


def diff_attention(q2d, kcat, vT, tri, lam_vec, g_rows, lam_init, seq):
    b = kcat.shape[0]
    nq = seq // ATT_T
    cols = 2 * ATT_T
    dv = DIFF_V_DIM + V_AUG
    hps = DIFF_HPS
    return pl.pallas_call(
        functools.partial(_diff_attn_kernel, lam_init=lam_init),
        out_shape=jax.ShapeDtypeStruct((b * seq, DIFF_V_W), jnp.bfloat16),
        grid=(b, DIFF_HEADS // hps, nq),
        in_specs=[
            pl.BlockSpec((ATT_T, hps * 2 * DIFF_QK_DIM), lambda bi, h, i: (bi * nq + i, h)),
            pl.BlockSpec((1, hps, nq, ATT_T, 2 * DIFF_QK_DIM), lambda bi, h, i: (bi, h, 0, 0, 0),
                         pipeline_mode=pl.Buffered(1)),
            pl.BlockSpec((1, hps, nq, dv, ATT_T), lambda bi, h, i: (bi, h, 0, 0, 0),
                         pipeline_mode=pl.Buffered(1)),
            pl.BlockSpec((2, ATT_T, ATT_T), lambda bi, h, i: (0, 0, 0)),
            pl.BlockSpec((4, DIFF_QK_DIM), lambda bi, h, i: (0, 0)),
            pl.BlockSpec((DIFF_V_DIM, LANES), lambda bi, h, i: (0, 0)),
        ],
        out_specs=pl.BlockSpec((ATT_T, hps * DIFF_V_DIM), lambda bi, h, i: (bi * nq + i, h)),
        scratch_shapes=[
            pltpu.VMEM((hps, 1, cols), jnp.float32),
            pltpu.VMEM((hps, dv, cols), jnp.float32),
            pltpu.VMEM((hps, ATT_T, cols), jnp.float32),
        ],
        compiler_params=_cparams(("parallel", "parallel", "arbitrary")),
        name="diff_attention",
    )(q2d, kcat, vT, tri, lam_vec, g_rows)


HALO = 2 * SUBLANES


def _ffn_kernel(x_ref, xh_ref, a_ref, ah_ref, wa_ref, g_ref, wg_ref, wv_ref, cwg_ref, cwv_ref, cbg_ref, cbv_ref,
                wo_ref, fg_ref, o_ref, h_s, acc_s, x1_s, *, tiles_per_seq, final_norm):
    i = pl.program_id(0)
    f = pl.program_id(1)

    @pl.when(f == 0)
    def _():
        g = g_ref[...]
        wa = wa_ref[...]
        x1 = x_ref[...] + _dot(a_ref[...], wa)
        x1_s[...] = x1
        h_s[HALO:, :] = _rms(x1, g).astype(jnp.bfloat16)
        hh = _rms(xh_ref[...] + _dot(ah_ref[...], wa), g)
        hh = jnp.where(i % tiles_per_seq == 0, 0.0, hh)
        h_s[:HALO, :] = hh.astype(jnp.bfloat16)
        acc_s[...] = jnp.zeros(acc_s.shape, jnp.float32)

    h = h_s[...]

    def conv(u, cw_ref, cb_ref):
        cw = cw_ref[...]
        c = cb_ref[...] + u[HALO:] * cw[2:3]
        c = c + pltpu.roll(u, 1, 0)[HALO:] * cw[1:2]
        c = c + pltpu.roll(u, 2, 0)[HALO:] * cw[0:1]
        return c

    gate = conv(_dot(h, wg_ref[...]), cwg_ref, cbg_ref)
    val = conv(_dot(h, wv_ref[...]), cwv_ref, cbv_ref)
    act = (gate * jax.nn.sigmoid(gate)) * val
    acc_s[...] += _dot(act.astype(jnp.bfloat16), wo_ref[...])

    @pl.when(f == pl.num_programs(1) - 1)
    def _():
        y = x1_s[...] + acc_s[...]
        if final_norm:
            y = _rms(y, fg_ref[...])
        o_ref[...] = y


def ffn(x2d, a2d, wa, g, wg, wv, cwg, cwv, cbg, cbv, wo, final_g, seq):
    m, d = x2d.shape
    ka = a2d.shape[1]
    tm = ROW_TILE
    nf = FFN_PAD // FFN_CHUNK
    tiles_per_seq = seq // tm
    final_norm = final_g is not None
    fg = (final_g if final_norm else g).reshape(1, d)
    kern = functools.partial(_ffn_kernel, tiles_per_seq=tiles_per_seq, final_norm=final_norm)
    return pl.pallas_call(
        kern,
        out_shape=jax.ShapeDtypeStruct((m, d), jnp.float32),
        grid=(m // tm, nf),
        in_specs=[
            pl.BlockSpec((tm, d), lambda i, f: (i, 0)),
            pl.BlockSpec((HALO, d), lambda i, f: (jnp.maximum(i * (tm // HALO) - 1, 0), 0)),
            pl.BlockSpec((tm, ka), lambda i, f: (i, 0)),
            pl.BlockSpec((HALO, ka), lambda i, f: (jnp.maximum(i * (tm // HALO) - 1, 0), 0)),
            pl.BlockSpec((ka, d), lambda i, f: (0, 0)),
            pl.BlockSpec((1, d), lambda i, f: (0, 0)),
            pl.BlockSpec((d, FFN_CHUNK), lambda i, f: (0, f)),
            pl.BlockSpec((d, FFN_CHUNK), lambda i, f: (0, f)),
            pl.BlockSpec((CONV_WIDTH, FFN_CHUNK), lambda i, f: (0, f)),
            pl.BlockSpec((CONV_WIDTH, FFN_CHUNK), lambda i, f: (0, f)),
            pl.BlockSpec((1, FFN_CHUNK), lambda i, f: (0, f)),
            pl.BlockSpec((1, FFN_CHUNK), lambda i, f: (0, f)),
            pl.BlockSpec((FFN_CHUNK, d), lambda i, f: (f, 0)),
            pl.BlockSpec((1, d), lambda i, f: (0, 0)),
        ],
        out_specs=pl.BlockSpec((tm, d), lambda i, f: (i, 0)),
        scratch_shapes=[
            pltpu.VMEM((HALO + tm, d), jnp.bfloat16),
            pltpu.VMEM((tm, d), jnp.float32),
            pltpu.VMEM((tm, d), jnp.float32),
        ],
        compiler_params=_cparams(("parallel", "arbitrary")),
        name="conv_glu_ffn",
    )(x2d, x2d, a2d, a2d, wa, g.reshape(1, d), wg, wv, cwg, cwv, cbg, cbv, wo, fg)


def _rope_tables(seq):
    half = ROT_DIM // 2
    inv = ROPE_THETA ** (-jnp.arange(half, dtype=jnp.float32) / half)
    ang = jnp.arange(seq).astype(jnp.float32)[:, None] * inv[None, :]
    cos, sin = jnp.cos(ang), jnp.sin(ang)
    ones = jnp.ones((seq, HEAD_DIM - ROT_DIM), jnp.float32)
    zeros = jnp.zeros((seq, HEAD_DIM - ROT_DIM), jnp.float32)
    z8 = jnp.zeros((seq, half), jnp.float32)
    c = jnp.concatenate([cos, cos, ones], axis=1)
    s_lo = jnp.concatenate([-sin, z8, zeros], axis=1)
    s_hi = jnp.concatenate([z8, sin, zeros], axis=1)
    tab = jnp.stack([c, s_lo, s_hi])
    return jnp.concatenate([tab] * (LANES // HEAD_DIM), axis=2)


def _triangle_biases():
    rk = jnp.arange(ATT_T)[:, None]
    rq = jnp.arange(ATT_T)[None, :]
    causal = jnp.where(rk <= rq, 0.0, MASK_BIAS)
    lower = jnp.where(rk > rq, 0.0, MASK_BIAS)
    return jnp.stack([causal, lower]).astype(jnp.float32)


def _pad_cols(w, n):
    return jnp.pad(w, ((0, 0), (0, n - w.shape[1])))


def _vT_aug(v, bsz, seq, heads, dv):
    nt = seq // ATT_T
    vT = v.reshape(bsz, nt, ATT_T, heads, dv).transpose(0, 3, 1, 4, 2)
    extra = jnp.zeros((bsz, heads, nt, V_AUG, ATT_T), v.dtype).at[:, :, :, 0, :].set(1.0)
    return jnp.concatenate([vT, extra], axis=3)


def _nsa_layer(x2d, bsz, seq, g, w_in, w_out, pe, w1, b1, w2, rope_tab, tri):
    bf = jnp.bfloat16
    hk, hd = NSA_KV_HEADS, HEAD_DIM
    wq = w_in[:, :NSA_Q_W]
    wkv = [w_in[:, NSA_Q_W + k * NSA_KV_W:NSA_Q_W + (k + 1) * NSA_KV_W] for k in range(6)]
    wg = _pad_cols(w_in[:, NSA_Q_W + 6 * NSA_KV_W:], COL_TILE)
    w_all = jnp.concatenate([wq] + wkv + [wg], axis=1).astype(bf)
    nq_t = NSA_Q_W // COL_TILE
    modes = ([MODE_BOTH_SCALE] * nq_t
             + [MODE_PLAIN, MODE_PLAIN, MODE_ROPE, MODE_PLAIN, MODE_ROPE, MODE_PLAIN, MODE_SIGMOID])
    proj2d = norm_linear(x2d, g, w_all, modes, rope_tab, seq)
    proj = proj2d.reshape(bsz, seq, -1)

    off = 2 * NSA_Q_W
    kc, vc, ks, vs, kw, vw = (proj[..., off + k * NSA_KV_W:off + (k + 1) * NSA_KV_W] for k in range(6))
    off += 6 * NSA_KV_W
    gates = proj[..., off:off + 3 * NSA_HEADS]

    nch = seq // CMP_STRIDE
    def chunked(a):
        return a.reshape(bsz, nch, CMP_STRIDE, hk, hd).transpose(0, 3, 1, 2, 4).reshape(bsz, hk, nch, CMP_STRIDE * hd)
    chunks = jnp.stack([chunked(kc), chunked(vc)])
    pe_flat8 = jnp.broadcast_to(pe.reshape(2, 1, CMP_BLOCK * hd), (2, 8, CMP_BLOCK * hd)).astype(bf)
    cmp = compress(chunks, pe_flat8, w1.astype(bf), b1.reshape(2, 1, PHI_HIDDEN), w2.astype(bf))
    kcmp = cmp[0]
    vcmpT = cmp[1].transpose(0, 1, 3, 2)

    n_sel = seq // SEL_BLOCK
    ci = jnp.arange(nch)[None, :]
    sj = jnp.arange(n_sel)[:, None]
    ovT = ((ci * CMP_STRIDE < (sj + 1) * SEL_BLOCK) & (ci * CMP_STRIDE + CMP_BLOCK > sj * SEL_BLOCK)
           & (ci < nch - 1)).astype(bf)

    nt = seq // ATT_T
    ks_h = ks.reshape(bsz, seq, hk, hd).transpose(0, 2, 1, 3)
    onehot = (jnp.arange(seq)[:, None] // SEL_BLOCK == jnp.arange(n_sel)[None, :]).astype(bf)
    kaug = jnp.concatenate(
        [ks_h, jnp.zeros((bsz, hk, seq, LANES - hd), bf),
         jnp.broadcast_to(onehot, (bsz, hk, seq, n_sel))], axis=-1)
    kaug = kaug.reshape(bsz, hk, nt, ATT_T, LANES + n_sel)
    vsT = _vT_aug(vs, bsz, seq, hk, hd)
    kw_t = kw.reshape(bsz, nt, ATT_T, hk, hd).transpose(0, 3, 1, 2, 4)
    vwT = _vT_aug(vw, bsz, seq, hk, hd)
    gates_t = gates.reshape(bsz, nt, ATT_T, hk, NSA_GROUP, 3).transpose(0, 3, 1, 5, 4, 2)
    gates_t = gates_t.reshape(bsz, hk, nt, 3, NSA_GROUP * ATT_T).astype(jnp.float32)

    o = nsa_attention(proj2d, kcmp, vcmpT, ovT, kaug, vsT, kw_t, vwT, tri, gates_t, seq)
    return o, w_out.astype(bf)


def _shared_kv(x2d, bsz, seq, g, kv_w, rope_tab):
    nk = DIFF_Q_W // COL_TILE
    nv = DIFF_V_W // COL_TILE
    kv = norm_linear(x2d, g, kv_w.astype(jnp.bfloat16), [MODE_ROPE] * nk + [MODE_PLAIN] * nv, rope_tab, seq)
    kv = kv.reshape(bsz, seq, -1)
    k = kv[..., :DIFF_Q_W].reshape(bsz, seq // ATT_T, ATT_T, DIFF_HEADS, 2 * DIFF_QK_DIM)
    kcat = k.transpose(0, 3, 1, 2, 4)
    vT = _vT_aug(kv[..., DIFF_Q_W:], bsz, seq, DIFF_HEADS, DIFF_V_DIM)
    return kcat, vT


def _diff_layer(x2d, bsz, seq, g, w_q, lam_vec, subln_g, w_out, kcat, vT, lam_init, rope_tab, tri):
    bf = jnp.bfloat16
    q = norm_linear(x2d, g, w_q.astype(bf), [MODE_ROPE_SCALE] * (DIFF_Q_W // COL_TILE), rope_tab, seq)
    g_rows = jnp.broadcast_to(subln_g.astype(jnp.float32)[:, None], (DIFF_V_DIM, LANES))
    o = diff_attention(q, kcat, vT, tri, lam_vec.astype(jnp.float32), g_rows, lam_init, seq)
    return o, w_out.astype(bf)


def _ffn_layer(x2d, a2d, wa, seq, g, w_in, conv_w, conv_b, w_out, final_g):
    bf = jnp.bfloat16
    f = FFN_DIM
    wg = _pad_cols(w_in[:, :f], FFN_PAD).astype(bf)
    wv = _pad_cols(w_in[:, f:], FFN_PAD).astype(bf)
    cwg = _pad_cols(conv_w[:, :f], FFN_PAD)
    cwv = _pad_cols(conv_w[:, f:], FFN_PAD)
    cbg = _pad_cols(conv_b[None, :f], FFN_PAD)
    cbv = _pad_cols(conv_b[None, f:], FFN_PAD)
    wo = jnp.pad(w_out, ((0, FFN_PAD - f), (0, 0))).astype(bf)
    return ffn(x2d, a2d, wa, g, wg, wv, cwg, cwv, cbg, cbv, wo, final_g, seq)


def kernel(x, attn_norm_g, ffn_norm_g, nsa_w_in, nsa_w_out, nsa_phi_pe, nsa_phi_w1, nsa_phi_b1, nsa_phi_w2,
           kv_norm_g, kv_w, diff_w_q, diff_lambda, diff_subln_g, diff_w_out, ffn_w_in, ffn_conv_w, ffn_conv_b,
           ffn_w_out, final_norm_g):
    bsz, seq, d = x.shape
    x2d = x.reshape(bsz * seq, d)
    rope_tab = _rope_tables(seq)
    tri = _triangle_biases()
    kcat = vT = None
    for l in range(DEPTH):
        if l < N_A_LAYERS:
            a2d, wa = _nsa_layer(x2d, bsz, seq, attn_norm_g[l], nsa_w_in[l], nsa_w_out[l], nsa_phi_pe[l],
                             nsa_phi_w1[l], nsa_phi_b1[l], nsa_phi_w2[l], rope_tab, tri)
        else:
            j = l - N_A_LAYERS
            lam_init = 0.8 - 0.6 * math.exp(-0.3 * l)
            a2d, wa = _diff_layer(x2d, bsz, seq, attn_norm_g[l], diff_w_q[j], diff_lambda[j], diff_subln_g[j],
                              diff_w_out[j], kcat, vT, lam_init, rope_tab, tri)
        final_g = final_norm_g if l == DEPTH - 1 else None
        x2d = _ffn_layer(x2d, a2d, wa, seq, ffn_norm_g[l], ffn_w_in[l], ffn_conv_w[l], ffn_conv_b[l], ffn_w_out[l], final_g)
        if l == N_A_LAYERS - 1:
            kcat, vT = _shared_kv(x2d, bsz, seq, kv_norm_g, kv_w, rope_tab)
    return x2d.reshape(bsz, seq, d)
```
